```python
import jax, jax.numpy as jnp
from jax import lax
import numpy as np

D_MODEL = 2048
BATCH = 2
SEQ = 8192
DEPTH = 4
DEC_BATCH = 4
DEC_SEQ = 4096
PAST_LEN = 128

N_EVEN = (DEPTH + 1) // 2
N_ODD = DEPTH // 2
MIX_WIDTH = D_MODEL
A_WIDTH = MIX_WIDTH // 2
A_HEADS = 4
A_HEAD_DIM = A_WIDTH // A_HEADS
CHUNK = 128
B_WIDTH = MIX_WIDTH - A_WIDTH
POOL_WINDOWS = (2, 4, 8, 16)
B_GROUPS = len(POOL_WINDOWS)
B_GROUP_DIM = B_WIDTH // B_GROUPS
AB_IN = 2 * A_WIDTH + B_WIDTH
C_WIDTH = MIX_WIDTH // 2
C_KERNEL = 31
D_WIDTH = MIX_WIDTH - C_WIDTH
D_GROUPS = 4
D_GROUP_DIM = D_WIDTH // D_GROUPS
CD_IN = 2 * C_WIDTH + D_WIDTH
D_FF = 5632
FFN_KERNEL = 3
EPS = 1e-6

kernel_name = "hybrid_sgu_pool_conformer_fnet_encoder"


def rms_norm(x, g):
    xf = x.astype(jnp.float32)
    y = xf * lax.rsqrt(jnp.mean(xf * xf, axis=-1, keepdims=True) + EPS)
    return (y * g.astype(jnp.float32)).astype(x.dtype)


def layer_norm(x, g, b):
    xf = x.astype(jnp.float32)
    mu = jnp.mean(xf, axis=-1, keepdims=True)
    xc = xf - mu
    y = xc * lax.rsqrt(jnp.mean(xc * xc, axis=-1, keepdims=True) + EPS)
    return (y * g.astype(jnp.float32) + b.astype(jnp.float32)).astype(x.dtype)


def depthwise_conv(x, w, b):
    k = w.shape[0]
    y = lax.conv_general_dilated(
        x, w[:, None, :].astype(x.dtype), window_strides=(1,),
        padding=[(k // 2, k // 2)], dimension_numbers=("NWC", "WIO", "NWC"),
        feature_group_count=x.shape[-1])
    return y + b.astype(x.dtype)


def mixer_a(u, v, v_gain, w_s, b_s):
    u = jax.nn.gelu(u)
    v = rms_norm(jax.nn.gelu(v), v_gain)
    bsz, s, _ = v.shape
    vh = v.reshape(bsz, s // CHUNK, CHUNK, A_HEADS, A_HEAD_DIM)
    mixed = jnp.einsum("hpq,bcqhd->bcphd", w_s.astype(v.dtype), vh)
    mixed = mixed + b_s.T.astype(v.dtype)[None, None, :, :, None]
    return u * mixed.reshape(bsz, s, A_WIDTH)


def mixer_b(xb, w_g, scale):
    bsz, s, _ = xb.shape
    xf = xb.astype(jnp.float32)
    csum = jnp.concatenate(
        [jnp.zeros((bsz, 1, B_WIDTH), jnp.float32), jnp.cumsum(xf, axis=1)], axis=1)
    t = np.arange(s)
    outs = []
    for g, w in enumerate(POOL_WINDOWS):
        half = w // 2
        lo = np.maximum(t - half, 0)
        hi = np.minimum(t + half, s)
        cnt = (hi - lo).astype(np.float32)[None, :, None]
        sl = slice(g * B_GROUP_DIM, (g + 1) * B_GROUP_DIM)
        cs = csum[:, :, sl]
        outs.append((cs[:, hi] - cs[:, lo]) / cnt - xf[:, :, sl])
    pooled = jnp.stack(outs, axis=2).astype(xb.dtype)
    mixed = jnp.einsum("bsgd,gde->bsge", pooled, w_g.astype(xb.dtype)).reshape(bsz, s, B_WIDTH)
    return mixed * scale.astype(xb.dtype)


def mixer_c(h, conv_w, conv_b, ln_g, ln_b):
    a, gate = h[..., :C_WIDTH], h[..., C_WIDTH:]
    y = a * jax.nn.sigmoid(gate)
    y = depthwise_conv(y, conv_w, conv_b)
    return jax.nn.silu(layer_norm(y, ln_g, ln_b))


def mixer_d(xd):
    bsz, s, _ = xd.shape
    xf = xd.astype(jnp.float32).reshape(bsz, s, D_GROUPS, D_GROUP_DIM)
    y = jnp.real(jnp.fft.fft2(xf, axes=(1, 3), norm="ortho"))
    return y.astype(xd.dtype).reshape(bsz, s, D_WIDTH)


def conv_ffn(h, w_in, conv_w, conv_b, w_out):
    p = h @ w_in.astype(h.dtype)
    a, g = p[..., :D_FF], p[..., D_FF:]
    a = depthwise_conv(a, conv_w, conv_b)
    return (jax.nn.gelu(a) * g) @ w_out.astype(h.dtype)


def setup_inputs(seed: int = 0) -> dict:
    key = jax.random.key(seed)
    ks = jax.random.split(key, 32)
    f32 = jnp.float32
    nrm = lambda k, shape, scale: jax.random.normal(k, shape, f32) * scale
    res_scale = (2.0 * DEPTH) ** -0.5
    return {
        "x_prompt": nrm(ks[0], (BATCH, SEQ, D_MODEL), 1.0),
        "x_sample": nrm(ks[1], (DEC_BATCH, DEC_SEQ, D_MODEL), 1.0),
        "norm_mix": 1.0 + nrm(ks[2], (DEPTH, D_MODEL), 0.02),
        "norm_ffn": 1.0 + nrm(ks[3], (DEPTH, D_MODEL), 0.02),
        "norm_final": 1.0 + nrm(ks[4], (D_MODEL,), 0.02),
        "ab_w_in": nrm(ks[5], (N_EVEN, D_MODEL, AB_IN), D_MODEL ** -0.5),
        "ab_w_out": nrm(ks[6], (N_EVEN, MIX_WIDTH, D_MODEL), MIX_WIDTH ** -0.5 * res_scale),
        "a_v_gain": 1.0 + nrm(ks[7], (N_EVEN, A_WIDTH), 0.02),
        "a_w_s": nrm(ks[8], (N_EVEN, A_HEADS, CHUNK, CHUNK), CHUNK ** -0.5),
        "a_b_s": 1.0 + nrm(ks[9], (N_EVEN, A_HEADS, CHUNK), 0.02),
        "b_w_g": nrm(ks[10], (N_EVEN, B_GROUPS, B_GROUP_DIM, B_GROUP_DIM), B_GROUP_DIM ** -0.5),
        "b_scale": 1.0 + nrm(ks[11], (N_EVEN, B_WIDTH), 0.02),
        "cd_w_in": nrm(ks[12], (N_ODD, D_MODEL, CD_IN), D_MODEL ** -0.5),
        "cd_w_out": nrm(ks[13], (N_ODD, MIX_WIDTH, D_MODEL), MIX_WIDTH ** -0.5 * res_scale),
        "c_conv_w": nrm(ks[14], (N_ODD, C_KERNEL, C_WIDTH), C_KERNEL ** -0.5),
        "c_conv_b": nrm(ks[15], (N_ODD, C_WIDTH), 0.02),
        "c_ln_g": 1.0 + nrm(ks[16], (N_ODD, C_WIDTH), 0.02),
        "c_ln_b": nrm(ks[17], (N_ODD, C_WIDTH), 0.02),
        "f_w_in": nrm(ks[18], (DEPTH, D_MODEL, 2 * D_FF), D_MODEL ** -0.5),
        "f_conv_w": nrm(ks[19], (DEPTH, FFN_KERNEL, D_FF), FFN_KERNEL ** -0.5),
        "f_conv_b": nrm(ks[20], (DEPTH, D_FF), 0.02),
        "f_w_out": nrm(ks[21], (DEPTH, D_FF, D_MODEL), D_FF ** -0.5 * res_scale),
    }


def reference(x_prompt, x_sample, norm_mix, norm_ffn, norm_final,
              ab_w_in, ab_w_out, a_v_gain, a_w_s, a_b_s, b_w_g, b_scale,
              cd_w_in, cd_w_out, c_conv_w, c_conv_b, c_ln_g, c_ln_b,
              f_w_in, f_conv_w, f_conv_b, f_w_out):
    def trunk(x):
        for l in range(DEPTH):
            i = l // 2
            h = rms_norm(x, norm_mix[l])
            if l % 2 == 0:
                p = h @ ab_w_in[i].astype(h.dtype)
                u = p[..., :A_WIDTH]
                v = p[..., A_WIDTH:2 * A_WIDTH]
                xb = p[..., 2 * A_WIDTH:]
                ya = mixer_a(u, v, a_v_gain[i], a_w_s[i], a_b_s[i])
                yb = mixer_b(xb, b_w_g[i], b_scale[i])
                y = jnp.concatenate([ya, yb], axis=-1) @ ab_w_out[i].astype(h.dtype)
            else:
                p = h @ cd_w_in[i].astype(h.dtype)
                yc = mixer_c(p[..., :2 * C_WIDTH], c_conv_w[i], c_conv_b[i], c_ln_g[i], c_ln_b[i])
                yd = mixer_d(p[..., 2 * C_WIDTH:])
                y = jnp.concatenate([yc, yd], axis=-1) @ cd_w_out[i].astype(h.dtype)
            x = x + y
            h = rms_norm(x, norm_ffn[l])
            x = x + conv_ffn(h, f_w_in[l], f_conv_w[l], f_conv_b[l], f_w_out[l])
        return rms_norm(x, norm_final)

    y_prompt = trunk(x_prompt)
    y_sample = trunk(x_sample)
    return (y_prompt, y_sample)
```

```python
import functools

import numpy as np
import jax
import jax.numpy as jnp
from jax import lax
from jax.experimental import pallas as pl
from jax.experimental.pallas import tpu as pltpu

F32 = jnp.float32
BF16 = jnp.bfloat16
EPS = 1e-6

CHUNK = 128
A_HEADS = 4
POOL_WINDOWS = (2, 4, 8, 16)
GROUP_DIM = 256
DFT_N1 = 128

VMEM_LIMIT = 56 * 1024 * 1024
BF16_ROWS = 16
F32_ROWS = 8


def _rms(x, g):
    return x * lax.rsqrt(jnp.mean(x * x, axis=-1, keepdims=True) + EPS) * g


def _params(sem):
    return pltpu.CompilerParams(dimension_semantics=sem, vmem_limit_bytes=VMEM_LIMIT)


def _proj_in_kernel(x_ref, g_ref, w_ref, o_ref, h_ref):
    @pl.when(pl.program_id(1) == 0)
    def _():
        h_ref[...] = _rms(x_ref[...], g_ref[...]).astype(BF16)

    o_ref[...] = jnp.dot(h_ref[...], w_ref[...], preferred_element_type=F32)


def _proj_in(x, gain, w, layer, *, tm, tn):
    m, d = x.shape
    n = w.shape[2]
    return pl.pallas_call(
        _proj_in_kernel,
        grid=(m // tm, n // tn),
        in_specs=[
            pl.BlockSpec((tm, d), lambda i, j: (i, 0)),
            pl.BlockSpec((1, d), lambda i, j: (0, 0)),
            pl.BlockSpec((None, d, tn), lambda i, j: (layer, 0, j)),
        ],
        out_specs=pl.BlockSpec((tm, tn), lambda i, j: (i, j)),
        out_shape=jax.ShapeDtypeStruct((m, n), F32),
        scratch_shapes=[pltpu.VMEM((tm, d), BF16)],
        compiler_params=_params(("parallel", "arbitrary")),
        name="proj_in",
    )(x, gain.reshape(1, d), w)


def _proj_out_kernel(x_ref, y1_ref, y2_ref, w1_ref, w2_ref, o_ref):
    acc = jnp.dot(y1_ref[...], w1_ref[...], preferred_element_type=F32)
    acc += jnp.dot(y2_ref[...], w2_ref[...], preferred_element_type=F32)
    o_ref[...] = x_ref[...] + acc


def _proj_out(x, y1, y1_col, y2, y2_col, w, layer, *, tm):
    m, d = x.shape
    kh = w.shape[1] // 2
    return pl.pallas_call(
        _proj_out_kernel,
        grid=(m // tm,),
        in_specs=[
            pl.BlockSpec((tm, d), lambda i: (i, 0)),
            pl.BlockSpec((tm, kh), lambda i: (i, y1_col)),
            pl.BlockSpec((tm, kh), lambda i: (i, y2_col)),
            pl.BlockSpec((None, kh, d), lambda i: (layer, 0, 0)),
            pl.BlockSpec((None, kh, d), lambda i: (layer, 1, 0)),
        ],
        out_specs=pl.BlockSpec((tm, d), lambda i: (i, 0)),
        out_shape=jax.ShapeDtypeStruct((m, d), F32),
        compiler_params=_params(("parallel",)),
        name="proj_out",
    )(x, y1, y2, w, w)


def _mixer_even_kernel(u_ref, v_ref, xb_ref, xp_ref, xn_ref, vg_ref, ws_ref, bm_ref, wg_ref, sc_ref,
                       o_ref, xbuf, *, tm, tps, seq):
    it = pl.program_id(0) % tps
    aw = u_ref.shape[1]

    for c in range(tm // CHUNK):
        rows = slice(c * CHUNK, (c + 1) * CHUNK)
        vn = _rms(jax.nn.gelu(v_ref[rows, :]), vg_ref[...]).astype(BF16)
        for h in range(A_HEADS):
            cols = slice(h * GROUP_DIM, (h + 1) * GROUP_DIM)
            mixed = jnp.dot(ws_ref[h], vn[:, cols], preferred_element_type=F32)
            o_ref[rows, cols] = (jax.nn.gelu(u_ref[rows, cols]) * (mixed + bm_ref[:, cols])).astype(BF16)

    halo = F32_ROWS
    xbuf[halo:halo + tm, :] = xb_ref[...]
    xbuf[0:halo, :] = jnp.where(it == 0, 0.0, xp_ref[...])
    xbuf[halo + tm:2 * halo + tm, :] = jnp.where(it == tps - 1, 0.0, xn_ref[...])
    pos = it * tm + lax.broadcasted_iota(jnp.int32, (tm, GROUP_DIM), 0)
    for g, w in enumerate(POOL_WINDOWS):
        half = w // 2
        cols = slice(g * GROUP_DIM, (g + 1) * GROUP_DIM)
        s = xbuf[pl.ds(halo - half, tm), cols]
        for dlt in range(-half + 1, half):
            s = s + xbuf[pl.ds(halo + dlt, tm), cols]
        cnt = (jnp.minimum(pos + half, seq) - jnp.maximum(pos - half, 0)).astype(F32)
        pooled = (s / cnt - xb_ref[:, cols]).astype(BF16)
        mixed = jnp.dot(pooled, wg_ref[g], preferred_element_type=F32)
        o_ref[:, aw + g * GROUP_DIM:aw + (g + 1) * GROUP_DIM] = (mixed * sc_ref[:, cols]).astype(BF16)


def _mixer_even(p, v_gain, w_s, b_rows, w_g, scale, layer, seq, *, tm):
    m = p.shape[0]
    wdt = p.shape[1] // 3
    tps = seq // tm
    halo = F32_ROWS
    hb = tm // halo
    last = m // halo - 1
    kern = functools.partial(_mixer_even_kernel, tm=tm, tps=tps, seq=seq)
    return pl.pallas_call(
        kern,
        grid=(m // tm,),
        in_specs=[
            pl.BlockSpec((tm, wdt), lambda i: (i, 0)),
            pl.BlockSpec((tm, wdt), lambda i: (i, 1)),
            pl.BlockSpec((tm, wdt), lambda i: (i, 2)),
            pl.BlockSpec((halo, wdt), lambda i: (jnp.maximum(i * hb - 1, 0), 2)),
            pl.BlockSpec((halo, wdt), lambda i: (jnp.minimum((i + 1) * hb, last), 2)),
            pl.BlockSpec((None, 1, wdt), lambda i: (layer, 0, 0)),
            pl.BlockSpec((None, A_HEADS, CHUNK, CHUNK), lambda i: (layer, 0, 0, 0)),
            pl.BlockSpec((None, CHUNK, wdt), lambda i: (layer, 0, 0)),
            pl.BlockSpec((None, len(POOL_WINDOWS), GROUP_DIM, GROUP_DIM), lambda i: (layer, 0, 0, 0)),
            pl.BlockSpec((None, 1, wdt), lambda i: (layer, 0, 0)),
        ],
        out_specs=pl.BlockSpec((tm, 2 * wdt), lambda i: (i, 0)),
        out_shape=jax.ShapeDtypeStruct((m, 2 * wdt), BF16),
        scratch_shapes=[pltpu.VMEM((tm + 2 * halo, wdt), F32)],
        compiler_params=_params(("parallel",)),
        name="mixer_even",
    )(p, p, p, p, p, v_gain, w_s, b_rows, w_g, scale)


def _mixer_c_kernel(a_ref, gt_ref, ap_ref, gp_ref, an_ref, gn_ref, cw_ref, cb_ref, lg_ref, lb_ref,
                    o_ref, ybuf, *, tm, tps, rb):
    it = pl.program_id(0) % tps
    halo = ap_ref.shape[0]
    taps = cw_ref.shape[0]
    ybuf[halo:halo + tm, :] = a_ref[...] * jax.nn.sigmoid(gt_ref[...])
    ybuf[0:halo, :] = jnp.where(it == 0, 0.0, ap_ref[...] * jax.nn.sigmoid(gp_ref[...]))
    ybuf[halo + tm:2 * halo + tm, :] = jnp.where(it == tps - 1, 0.0, an_ref[...] * jax.nn.sigmoid(gn_ref[...]))
    first = halo - taps // 2
    for blk in range(tm // rb):
        r0 = blk * rb
        acc = cw_ref[0:1, :] * ybuf[pl.ds(r0 + first, rb), :]
        for k in range(1, taps):
            acc = acc + cw_ref[k:k + 1, :] * ybuf[pl.ds(r0 + first + k, rb), :]
        y = acc + cb_ref[...]
        yc = y - jnp.mean(y, axis=-1, keepdims=True)
        ln = yc * lax.rsqrt(jnp.mean(yc * yc, axis=-1, keepdims=True) + EPS) * lg_ref[...] + lb_ref[...]
        o_ref[r0:r0 + rb, :] = (ln * jax.nn.sigmoid(ln)).astype(BF16)


def _mixer_c(p, conv_w, conv_b, ln_g, ln_b, layer, seq, *, tm, rb=32):
    m = p.shape[0]
    wdt = p.shape[1] // 3
    taps = conv_w.shape[1]
    tps = seq // tm
    halo = 2 * F32_ROWS
    assert taps // 2 <= halo
    hb = tm // halo
    last = m // halo - 1
    prev = lambda i: jnp.maximum(i * hb - 1, 0)
    nxt = lambda i: jnp.minimum((i + 1) * hb, last)
    kern = functools.partial(_mixer_c_kernel, tm=tm, tps=tps, rb=rb)
    vec = pl.BlockSpec((None, 1, wdt), lambda i: (layer, 0, 0))
    return pl.pallas_call(
        kern,
        grid=(m // tm,),
        in_specs=[
            pl.BlockSpec((tm, wdt), lambda i: (i, 0)),
            pl.BlockSpec((tm, wdt), lambda i: (i, 1)),
            pl.BlockSpec((halo, wdt), lambda i: (prev(i), 0)),
            pl.BlockSpec((halo, wdt), lambda i: (prev(i), 1)),
            pl.BlockSpec((halo, wdt), lambda i: (nxt(i), 0)),
            pl.BlockSpec((halo, wdt), lambda i: (nxt(i), 1)),
            pl.BlockSpec((None, taps, wdt), lambda i: (layer, 0, 0)),
            vec, vec, vec,
        ],
        out_specs=pl.BlockSpec((tm, wdt), lambda i: (i, 0)),
        out_shape=jax.ShapeDtypeStruct((m, wdt), BF16),
        scratch_shapes=[pltpu.VMEM((tm + 2 * halo, wdt), F32)],
        compiler_params=_params(("parallel",)),
        name="mixer_c",
    )(p, p, p, p, p, p, conv_w, conv_b, ln_g, ln_b)


def _dft_tables(seq):
    n1 = DFT_N1
    n2 = seq // n1
    c = np.arange(GROUP_DIM)
    ang = 2.0 * np.pi * ((c[:, None] * c[None, :]) % GROUP_DIM) / GROUP_DIM
    fc = jnp.asarray(np.concatenate([np.cos(ang), -np.sin(ang)], axis=1), BF16)

    shape = (n2, n1, n1)
    r = lax.broadcasted_iota(jnp.int32, shape, 0)
    k1 = lax.broadcasted_iota(jnp.int32, shape, 1)
    j = lax.broadcasted_iota(jnp.int32, shape, 2)
    idx = (k1 * (n2 * j + r)) % seq
    a1 = idx.astype(F32) * (2.0 * np.pi / seq)
    gr, gi = jnp.cos(a1), -jnp.sin(a1)
    m1 = jnp.concatenate([jnp.concatenate([gr, -gi], axis=2),
                          jnp.concatenate([gi, gr], axis=2)], axis=1).astype(BF16)

    k2 = np.arange(n2)
    a2 = 2.0 * np.pi * ((k2[:, None] * k2[None, :]) % n2) / n2
    norm = 1.0 / np.sqrt(float(seq) * GROUP_DIM)
    w2 = jnp.asarray(np.concatenate([np.cos(a2), np.sin(a2)], axis=1) * norm, BF16)
    return fc, m1, w2


def _dft1_kernel(x_ref, fc_ref, m1_ref, o_ref, zbuf):
    n1 = x_ref.shape[0]
    x = x_ref[...].astype(BF16)
    for g in range(x_ref.shape[1] // GROUP_DIM):
        cols = slice(g * GROUP_DIM, (g + 1) * GROUP_DIM)
        zz = jnp.dot(x[:, cols], fc_ref[...], preferred_element_type=F32)
        zbuf[0:n1, cols] = zz[:, :GROUP_DIM].astype(BF16)
        zbuf[n1:2 * n1, cols] = zz[:, GROUP_DIM:].astype(BF16)
    y = jnp.dot(m1_ref[...], zbuf[...], preferred_element_type=F32)
    o_ref[0] = y[:n1].astype(BF16)
    o_ref[1] = y[n1:].astype(BF16)


def _dft2_kernel(w_ref, y_ref, o_ref):
    o_ref[...] = jnp.dot(w_ref[...], y_ref[...], preferred_element_type=F32).astype(BF16)


def _mixer_d(p, bsz, seq, tables, *, tc=8192):
    fc, m1, w2 = tables
    m = p.shape[0]
    wdt = p.shape[1] // 3
    n1 = DFT_N1
    n2 = seq // n1
    pv = p.reshape(bsz, n1, n2 * 3 * wdt)
    yp = pl.pallas_call(
        _dft1_kernel,
        grid=(bsz, n2),
        in_specs=[
            pl.BlockSpec((None, n1, wdt), lambda b, r: (b, 0, 3 * r + 2)),
            pl.BlockSpec((GROUP_DIM, 2 * GROUP_DIM), lambda b, r: (0, 0)),
            pl.BlockSpec((None, 2 * n1, 2 * n1), lambda b, r: (r, 0, 0)),
        ],
        out_specs=pl.BlockSpec((None, 2, None, n1, wdt), lambda b, r: (b, 0, r, 0, 0)),
        out_shape=jax.ShapeDtypeStruct((bsz, 2, n2, n1, wdt), BF16),
        scratch_shapes=[pltpu.VMEM((2 * n1, wdt), BF16)],
        compiler_params=_params(("parallel", "parallel")),
        name="dft_stage1",
    )(pv, fc, m1)
    ypv = yp.reshape(bsz, 2 * n2, n1 * wdt)
    out = pl.pallas_call(
        _dft2_kernel,
        grid=(bsz, n1 * wdt // tc),
        in_specs=[
            pl.BlockSpec((n2, 2 * n2), lambda b, c: (0, 0)),
            pl.BlockSpec((None, 2 * n2, tc), lambda b, c: (b, 0, c)),
        ],
        out_specs=pl.BlockSpec((None, n2, tc), lambda b, c: (b, 0, c)),
        out_shape=jax.ShapeDtypeStruct((bsz, n2, n1 * wdt), BF16),
        compiler_params=_params(("parallel", "parallel")),
        name="dft_stage2",
    )(w2, ypv)
    return out.reshape(m, wdt)


def _ffn_kernel(xp_ref, x_ref, xn_ref, g_ref, wa_ref, wg_ref, cw_ref, cb_ref, wo_ref, gf_ref,
                o_ref, hbuf, abuf, acc, *, tm, tps, final):
    it = pl.program_id(0) % tps
    f = pl.program_id(1)
    halo = xp_ref.shape[0]

    @pl.when(f == 0)
    def _():
        g = g_ref[...]
        hbuf[halo:halo + tm, :] = _rms(x_ref[...], g).astype(BF16)
        hbuf[0:halo, :] = jnp.where(it == 0, 0.0, _rms(xp_ref[...], g)).astype(BF16)
        hbuf[halo + tm:2 * halo + tm, :] = jnp.where(it == tps - 1, 0.0, _rms(xn_ref[...], g)).astype(BF16)
        acc[...] = jnp.zeros_like(acc)

    abuf[...] = jnp.dot(hbuf[...], wa_ref[...], preferred_element_type=F32)
    gate = jnp.dot(hbuf[halo:halo + tm, :], wg_ref[...], preferred_element_type=F32)
    a = (cw_ref[0:1, :] * abuf[pl.ds(halo - 1, tm), :]
         + cw_ref[1:2, :] * abuf[pl.ds(halo, tm), :]
         + cw_ref[2:3, :] * abuf[pl.ds(halo + 1, tm), :]) + cb_ref[...]
    act = (jax.nn.gelu(a) * gate).astype(BF16)
    acc[...] += jnp.dot(act, wo_ref[...], preferred_element_type=F32)

    @pl.when(f == pl.num_programs(1) - 1)
    def _():
        y = x_ref[...] + acc[...]
        if final:
            y = _rms(y, gf_ref[...])
        o_ref[...] = y


def _ffn(x, gain, w_in, conv_w, conv_b, w_out, final_gain, layer, seq, *, tm, tf, final):
    m, d = x.shape
    dff = w_out.shape[1]
    nf = dff // tf
    tps = seq // tm
    halo = BF16_ROWS
    hb = tm // halo
    last = m // halo - 1
    kern = functools.partial(_ffn_kernel, tm=tm, tps=tps, final=final)
    return pl.pallas_call(
        kern,
        grid=(m // tm, nf),
        in_specs=[
            pl.BlockSpec((halo, d), lambda i, f: (jnp.maximum(i * hb - 1, 0), 0)),
            pl.BlockSpec((tm, d), lambda i, f: (i, 0)),
            pl.BlockSpec((halo, d), lambda i, f: (jnp.minimum((i + 1) * hb, last), 0)),
            pl.BlockSpec((None, 1, d), lambda i, f: (layer, 0, 0)),
            pl.BlockSpec((None, d, tf), lambda i, f: (layer, 0, f)),
            pl.BlockSpec((None, d, tf), lambda i, f: (layer, 0, f + nf)),
            pl.BlockSpec((None, conv_w.shape[1], tf), lambda i, f: (layer, 0, f)),
            pl.BlockSpec((None, 1, tf), lambda i, f: (layer, 0, f)),
            pl.BlockSpec((None, tf, d), lambda i, f: (layer, f, 0)),
            pl.BlockSpec((1, d), lambda i, f: (0, 0)),
        ],
        out_specs=pl.BlockSpec((tm, d), lambda i, f: (i, 0)),
        out_shape=jax.ShapeDtypeStruct((m, d), F32),
        scratch_shapes=[
            pltpu.VMEM((tm + 2 * halo, d), BF16),
            pltpu.VMEM((tm + 2 * halo, tf), F32),
            pltpu.VMEM((tm, d), F32),
        ],
        compiler_params=_params(("parallel", "arbitrary")),
        name="ffn",
    )(x, x, x, gain, w_in, w_in, conv_w, conv_b, w_out, final_gain)


def kernel(x_prompt, x_sample, norm_mix, norm_ffn, norm_final, ab_w_in, ab_w_out, a_v_gain, a_w_s, a_b_s,
           b_w_g, b_scale, cd_w_in, cd_w_out, c_conv_w, c_conv_b, c_ln_g, c_ln_b, f_w_in, f_conv_w,
           f_conv_b, f_w_out):
    depth, d = norm_mix.shape
    row = lambda v: v.reshape(v.shape[0], 1, v.shape[1])
    ab_w_in_b, ab_w_out_b = ab_w_in.astype(BF16), ab_w_out.astype(BF16)
    cd_w_in_b, cd_w_out_b = cd_w_in.astype(BF16), cd_w_out.astype(BF16)
    f_w_in_b, f_w_out_b = f_w_in.astype(BF16), f_w_out.astype(BF16)
    a_w_s_b, b_w_g_b = a_w_s.astype(BF16), b_w_g.astype(BF16)
    b_rows = jnp.repeat(jnp.swapaxes(a_b_s, 1, 2), GROUP_DIM, axis=2)
    norm_ffn_r, f_conv_b_r = row(norm_ffn), row(f_conv_b)
    a_v_gain_r, b_scale_r = row(a_v_gain), row(b_scale)
    c_conv_b_r, c_ln_g_r, c_ln_b_r = row(c_conv_b), row(c_ln_g), row(c_ln_b)
    final_gain = norm_final.reshape(1, d)

    def trunk(x):
        bsz, seq, _ = x.shape
        h = x.reshape(bsz * seq, d)
        tables = _dft_tables(seq) if depth > 1 else None
        for l in range(depth):
            i = l // 2
            if l % 2 == 0:
                p = _proj_in(h, norm_mix[l], ab_w_in_b, i, tm=512, tn=1024)
                y = _mixer_even(p, a_v_gain_r, a_w_s_b, b_rows, b_w_g_b, b_scale_r, i, seq, tm=256)
                h = _proj_out(h, y, 0, y, 1, ab_w_out_b, i, tm=512)
            else:
                p = _proj_in(h, norm_mix[l], cd_w_in_b, i, tm=512, tn=1024)
                yc = _mixer_c(p, c_conv_w, c_conv_b_r, c_ln_g_r, c_ln_b_r, i, seq, tm=256)
                yd = _mixer_d(p, bsz, seq, tables)
                h = _proj_out(h, yc, 0, yd, 0, cd_w_out_b, i, tm=512)
            h = _ffn(h, norm_ffn_r, f_w_in_b, f_conv_w, f_conv_b_r, f_w_out_b, final_gain, l, seq,
                     tm=512, tf=512, final=(l == depth - 1))
        return h.reshape(bsz, seq, d)

    return trunk(x_prompt), trunk(x_sample)
```

```python
import functools

import numpy as np
import jax
import jax.numpy as jnp
from jax import lax
from jax.experimental import pallas as pl
from jax.experimental.pallas import tpu as pltpu

F32 = jnp.float32
BF16 = jnp.bfloat16
EPS = 1e-6

CHUNK = 128
A_HEADS = 4
POOL_WINDOWS = (2, 4, 8, 16)
GROUP_DIM = 256
DFT_N1 = 128

VMEM_LIMIT = 56 * 1024 * 1024
BF16_ROWS = 16
F32_ROWS = 8


def _rms(x, g):
    return x * lax.rsqrt(jnp.mean(x * x, axis=-1, keepdims=True) + EPS) * g


def _params(sem):
    return pltpu.CompilerParams(dimension_semantics=sem, vmem_limit_bytes=VMEM_LIMIT)


def _proj_in_kernel(x_ref, g_ref, w_ref, o_ref, h_ref):
    @pl.when(pl.program_id(1) == 0)
    def _():
        h_ref[...] = _rms(x_ref[...], g_ref[...]).astype(BF16)

    o_ref[...] = jnp.dot(h_ref[...], w_ref[...], preferred_element_type=F32)


def _proj_in(x, gain, w, layer, *, tm, tn):
    m, d = x.shape
    n = w.shape[2]
    return pl.pallas_call(
        _proj_in_kernel,
        grid=(m // tm, n // tn),
        in_specs=[
            pl.BlockSpec((tm, d), lambda i, j: (i, 0)),
            pl.BlockSpec((1, d), lambda i, j: (0, 0)),
            pl.BlockSpec((None, d, tn), lambda i, j: (layer, 0, j)),
        ],
        out_specs=pl.BlockSpec((tm, tn), lambda i, j: (i, j)),
        out_shape=jax.ShapeDtypeStruct((m, n), F32),
        scratch_shapes=[pltpu.VMEM((tm, d), BF16)],
        compiler_params=_params(("parallel", "arbitrary")),
        name="proj_in",
    )(x, gain.reshape(1, d), w)


def _proj_out_kernel(x_ref, y1_ref, y2_ref, w1_ref, w2_ref, o_ref):
    acc = jnp.dot(y1_ref[...].astype(BF16), w1_ref[...], preferred_element_type=F32)
    acc += jnp.dot(y2_ref[...].astype(BF16), w2_ref[...], preferred_element_type=F32)
    o_ref[...] = x_ref[...] + acc


def _proj_out(x, y1, y1_col, y2, y2_col, w, layer, *, tm):
    m, d = x.shape
    kh = w.shape[1] // 2
    return pl.pallas_call(
        _proj_out_kernel,
        grid=(m // tm,),
        in_specs=[
            pl.BlockSpec((tm, d), lambda i: (i, 0)),
            pl.BlockSpec((tm, kh), lambda i: (i, y1_col)),
            pl.BlockSpec((tm, kh), lambda i: (i, y2_col)),
            pl.BlockSpec((None, kh, d), lambda i: (layer, 0, 0)),
            pl.BlockSpec((None, kh, d), lambda i: (layer, 1, 0)),
        ],
        out_specs=pl.BlockSpec((tm, d), lambda i: (i, 0)),
        out_shape=jax.ShapeDtypeStruct((m, d), F32),
        compiler_params=_params(("parallel",)),
        name="proj_out",
    )(x, y1, y2, w, w)


def _mixer_even_kernel(u_ref, v_ref, xb_ref, xp_ref, xn_ref, vg_ref, ws_ref, bm_ref, wg_ref, sc_ref,
                       o_ref, xbuf, *, tm, tps, seq):
    it = pl.program_id(0) % tps
    aw = u_ref.shape[1]

    for c in range(tm // CHUNK):
        rows = slice(c * CHUNK, (c + 1) * CHUNK)
        vn = _rms(jax.nn.gelu(v_ref[rows, :]), vg_ref[...]).astype(BF16)
        for h in range(A_HEADS):
            cols = slice(h * GROUP_DIM, (h + 1) * GROUP_DIM)
            mixed = jnp.dot(ws_ref[h], vn[:, cols], preferred_element_type=F32)
            o_ref[rows, cols] = (jax.nn.gelu(u_ref[rows, cols]) * (mixed + bm_ref[:, cols])).astype(BF16)

    halo = F32_ROWS
    xbuf[halo:halo + tm, :] = xb_ref[...]
    xbuf[0:halo, :] = jnp.where(it == 0, 0.0, xp_ref[...])
    xbuf[halo + tm:2 * halo + tm, :] = jnp.where(it == tps - 1, 0.0, xn_ref[...])
    pos = it * tm + lax.broadcasted_iota(jnp.int32, (tm, GROUP_DIM), 0)
    for g, w in enumerate(POOL_WINDOWS):
        half = w // 2
        cols = slice(g * GROUP_DIM, (g + 1) * GROUP_DIM)
        s = xbuf[pl.ds(halo - half, tm), cols]
        for dlt in range(-half + 1, half):
            s = s + xbuf[pl.ds(halo + dlt, tm), cols]
        cnt = (jnp.minimum(pos + half, seq) - jnp.maximum(pos - half, 0)).astype(F32)
        pooled = (s / cnt - xb_ref[:, cols]).astype(BF16)
        mixed = jnp.dot(pooled, wg_ref[g], preferred_element_type=F32)
        o_ref[:, aw + g * GROUP_DIM:aw + (g + 1) * GROUP_DIM] = (mixed * sc_ref[:, cols]).astype(BF16)


def _mixer_even(p, v_gain, w_s, b_rows, w_g, scale, layer, seq, *, tm):
    m = p.shape[0]
    wdt = p.shape[1] // 3
    tps = seq // tm
    halo = F32_ROWS
    hb = tm // halo
    last = m // halo - 1
    kern = functools.partial(_mixer_even_kernel, tm=tm, tps=tps, seq=seq)
    return pl.pallas_call(
        kern,
        grid=(m // tm,),
        in_specs=[
            pl.BlockSpec((tm, wdt), lambda i: (i, 0)),
            pl.BlockSpec((tm, wdt), lambda i: (i, 1)),
            pl.BlockSpec((tm, wdt), lambda i: (i, 2)),
            pl.BlockSpec((halo, wdt), lambda i: (jnp.maximum(i * hb - 1, 0), 2)),
            pl.BlockSpec((halo, wdt), lambda i: (jnp.minimum((i + 1) * hb, last), 2)),
            pl.BlockSpec((None, 1, wdt), lambda i: (layer, 0, 0)),
            pl.BlockSpec((None, A_HEADS, CHUNK, CHUNK), lambda i: (layer, 0, 0, 0)),
            pl.BlockSpec((None, CHUNK, wdt), lambda i: (layer, 0, 0)),
            pl.BlockSpec((None, len(POOL_WINDOWS), GROUP_DIM, GROUP_DIM), lambda i: (layer, 0, 0, 0)),
            pl.BlockSpec((None, 1, wdt), lambda i: (layer, 0, 0)),
        ],
        out_specs=pl.BlockSpec((tm, 2 * wdt), lambda i: (i, 0)),
        out_shape=jax.ShapeDtypeStruct((m, 2 * wdt), BF16),
        scratch_shapes=[pltpu.VMEM((tm + 2 * halo, wdt), F32)],
        compiler_params=_params(("parallel",)),
        name="mixer_even",
    )(p, p, p, p, p, v_gain, w_s, b_rows, w_g, scale)


def _mixer_c_kernel(a_ref, gt_ref, ap_ref, gp_ref, an_ref, gn_ref, cw_ref, cb_ref, lg_ref, lb_ref,
                    o_ref, ysh, *, tm, tps):
    it = pl.program_id(0) % tps
    halo = ap_ref.shape[0]
    taps = cw_ref.shape[0]
    rows = F32_ROWS
    ysh[0, halo:halo + tm, :] = a_ref[...] * jax.nn.sigmoid(gt_ref[...])
    ysh[0, 0:halo, :] = jnp.where(it == 0, 0.0, ap_ref[...] * jax.nn.sigmoid(gp_ref[...]))
    ysh[0, halo + tm:2 * halo + tm, :] = jnp.where(it == tps - 1, 0.0, an_ref[...] * jax.nn.sigmoid(gn_ref[...]))
    span = tm + 2 * halo - rows
    for r in range(1, rows):
        ysh[r, 0:span, :] = ysh[0, pl.ds(r, span), :]
    first = halo - taps // 2
    grp = BF16_ROWS
    for blk in range(tm // grp):
        parts = []
        for sub in range(grp // rows):
            r0 = blk * grp + sub * rows
            acc = None
            for k in range(taps):
                q, r = divmod(first + k, rows)
                term = cw_ref[k] * ysh[r, pl.ds(r0 + q * rows, rows), :]
                acc = term if acc is None else acc + term
            parts.append(acc)
        y = jnp.concatenate(parts, axis=0) + cb_ref[...]
        yc = y - jnp.mean(y, axis=-1, keepdims=True)
        ln = yc * lax.rsqrt(jnp.mean(yc * yc, axis=-1, keepdims=True) + EPS) * lg_ref[...] + lb_ref[...]
        o_ref[blk * grp:(blk + 1) * grp, :] = (ln * jax.nn.sigmoid(ln)).astype(BF16)


def _mixer_c(p, conv_w, conv_b, ln_g, ln_b, layer, seq, *, tm):
    m = p.shape[0]
    wdt = p.shape[1] // 3
    taps = conv_w.shape[1]
    tps = seq // tm
    halo = 2 * F32_ROWS
    assert taps // 2 <= halo
    hb = tm // halo
    last = m // halo - 1
    prev = lambda i: jnp.maximum(i * hb - 1, 0)
    nxt = lambda i: jnp.minimum((i + 1) * hb, last)
    kern = functools.partial(_mixer_c_kernel, tm=tm, tps=tps)
    vec = pl.BlockSpec((None, 1, wdt), lambda i: (layer, 0, 0))
    return pl.pallas_call(
        kern,
        grid=(m // tm,),
        in_specs=[
            pl.BlockSpec((tm, wdt), lambda i: (i, 0)),
            pl.BlockSpec((tm, wdt), lambda i: (i, 1)),
            pl.BlockSpec((halo, wdt), lambda i: (prev(i), 0)),
            pl.BlockSpec((halo, wdt), lambda i: (prev(i), 1)),
            pl.BlockSpec((halo, wdt), lambda i: (nxt(i), 0)),
            pl.BlockSpec((halo, wdt), lambda i: (nxt(i), 1)),
            pl.BlockSpec((None, taps, F32_ROWS, wdt), lambda i: (layer, 0, 0, 0)),
            vec, vec, vec,
        ],
        out_specs=pl.BlockSpec((tm, wdt), lambda i: (i, 0)),
        out_shape=jax.ShapeDtypeStruct((m, wdt), BF16),
        scratch_shapes=[pltpu.VMEM((F32_ROWS, tm + 2 * halo, wdt), F32)],
        compiler_params=_params(("parallel",)),
        name="mixer_c",
    )(p, p, p, p, p, p, conv_w, conv_b, ln_g, ln_b)


def _dft_tables(seq):
    n1 = DFT_N1
    n2 = seq // n1
    c = np.arange(GROUP_DIM)
    ang = 2.0 * np.pi * ((c[:, None] * c[None, :]) % GROUP_DIM) / GROUP_DIM
    fc = jnp.asarray(np.concatenate([np.cos(ang), -np.sin(ang)], axis=1), BF16)

    shape = (n2, n1, n1)
    r = lax.broadcasted_iota(jnp.int32, shape, 0)
    k1 = lax.broadcasted_iota(jnp.int32, shape, 1)
    j = lax.broadcasted_iota(jnp.int32, shape, 2)
    idx = (k1 * (n2 * j + r)) % seq
    a1 = idx.astype(F32) * (2.0 * np.pi / seq)
    gr, gi = jnp.cos(a1), -jnp.sin(a1)
    m1 = jnp.concatenate([jnp.concatenate([gr, -gi], axis=2),
                          jnp.concatenate([gi, gr], axis=2)], axis=1).astype(BF16)

    k2 = np.arange(n2)
    a2 = 2.0 * np.pi * ((k2[:, None] * k2[None, :]) % n2) / n2
    norm = 1.0 / np.sqrt(float(seq) * GROUP_DIM)
    wc = jnp.asarray(np.cos(a2) * norm, BF16)
    ws = jnp.asarray(np.sin(a2) * norm, BF16)
    return fc, m1, wc, ws


def _dft1_kernel(x_ref, fc_ref, m1_ref, o_ref, zbuf):
    n1, rt, wdt = x_ref.shape
    xt = pltpu.einshape("jrc->rjc", x_ref[...])
    for rr in range(rt):
        x = xt[rr].astype(BF16)
        for g in range(x_ref.shape[2] // GROUP_DIM):
            cols = slice(g * GROUP_DIM, (g + 1) * GROUP_DIM)
            zz = jnp.dot(x[:, cols], fc_ref[...], preferred_element_type=F32)
            zbuf[0:n1, cols] = zz[:, :GROUP_DIM].astype(BF16)
            zbuf[n1:2 * n1, cols] = zz[:, GROUP_DIM:].astype(BF16)
        y = jnp.dot(m1_ref[rr], zbuf[...], preferred_element_type=F32)
        o_ref[0, rr] = y[:n1]
        o_ref[1, rr] = y[n1:]


def _dft2_kernel(wc_ref, ws_ref, y_ref, o_ref):
    rt = y_ref.shape[2]
    yr_t = pltpu.einshape("rkc->krc", y_ref[0])
    yi_t = pltpu.einshape("rkc->krc", y_ref[1])
    res = [jnp.dot(wc_ref[...], yr_t[kk].astype(BF16), preferred_element_type=F32)
           + jnp.dot(ws_ref[...], yi_t[kk].astype(BF16), preferred_element_type=F32) for kk in range(rt)]
    o_ref[...] = pltpu.einshape("krc->rkc", jnp.stack(res, axis=0))


def _mixer_d(p, bsz, seq, tables):
    fc, m1, wc, ws = tables
    m = p.shape[0]
    wdt = p.shape[1] // 3
    n1 = DFT_N1
    n2 = seq // n1
    rt = F32_ROWS
    pv = p.reshape(bsz, n1, n2 // rt, rt, 3 * wdt)
    yp = pl.pallas_call(
        _dft1_kernel,
        grid=(bsz, n2 // rt),
        in_specs=[
            pl.BlockSpec((None, n1, None, rt, wdt), lambda b, r: (b, 0, r, 0, 2)),
            pl.BlockSpec((GROUP_DIM, 2 * GROUP_DIM), lambda b, r: (0, 0)),
            pl.BlockSpec((rt, 2 * n1, 2 * n1), lambda b, r: (r, 0, 0)),
        ],
        out_specs=pl.BlockSpec((None, 2, rt, n1, wdt), lambda b, r: (b, 0, r, 0, 0)),
        out_shape=jax.ShapeDtypeStruct((bsz, 2, n2, n1, wdt), F32),
        scratch_shapes=[pltpu.VMEM((2 * n1, wdt), BF16)],
        compiler_params=_params(("parallel", "parallel")),
        name="dft_stage1",
    )(pv, fc, m1)
    out = pl.pallas_call(
        _dft2_kernel,
        grid=(bsz, n1 // rt),
        in_specs=[
            pl.BlockSpec((n2, n2), lambda b, c: (0, 0)),
            pl.BlockSpec((n2, n2), lambda b, c: (0, 0)),
            pl.BlockSpec((None, 2, n2, rt, wdt), lambda b, c: (b, 0, 0, c, 0)),
        ],
        out_specs=pl.BlockSpec((None, n2, rt, wdt), lambda b, c: (b, 0, c, 0)),
        out_shape=jax.ShapeDtypeStruct((bsz, n2, n1, wdt), F32),
        compiler_params=_params(("parallel", "parallel")),
        name="dft_stage2",
    )(wc, ws, yp)
    return out.reshape(m, wdt)


def _ffn_kernel(xp_ref, x_ref, xn_ref, g_ref, wa_ref, wg_ref, cw_ref, cb_ref, wo_ref, gf_ref,
                o_ref, hbuf, abuf, *, tm, tps, final):
    it = pl.program_id(0) % tps
    f = pl.program_id(1)
    halo = xp_ref.shape[0]

    @pl.when(f == 0)
    def _():
        g = g_ref[...]
        x = x_ref[...]
        hbuf[halo:halo + tm, :] = _rms(x, g).astype(BF16)
        hbuf[0:halo, :] = jnp.where(it == 0, 0.0, _rms(xp_ref[...], g)).astype(BF16)
        hbuf[halo + tm:2 * halo + tm, :] = jnp.where(it == tps - 1, 0.0, _rms(xn_ref[...], g)).astype(BF16)
        o_ref[...] = x

    abuf[...] = jnp.dot(hbuf[...], wa_ref[...], preferred_element_type=F32)
    gate = jnp.dot(hbuf[halo:halo + tm, :], wg_ref[...], preferred_element_type=F32)
    a = (cw_ref[0:1, :] * abuf[pl.ds(halo - 1, tm), :]
         + cw_ref[1:2, :] * abuf[pl.ds(halo, tm), :]
         + cw_ref[2:3, :] * abuf[pl.ds(halo + 1, tm), :]) + cb_ref[...]
    act = (jax.nn.gelu(a) * gate).astype(BF16)
    o_ref[...] += jnp.dot(act, wo_ref[...], preferred_element_type=F32)

    if final:
        @pl.when(f == pl.num_programs(1) - 1)
        def _():
            o_ref[...] = _rms(o_ref[...], gf_ref[...])


def _ffn(x, gain, w_in, conv_w, conv_b, w_out, final_gain, layer, seq, *, tm, tf, final):
    m, d = x.shape
    dff = w_out.shape[1]
    nf = dff // tf
    tps = seq // tm
    halo = BF16_ROWS
    hb = tm // halo
    last = m // halo - 1
    kern = functools.partial(_ffn_kernel, tm=tm, tps=tps, final=final)
    return pl.pallas_call(
        kern,
        grid=(m // tm, nf),
        in_specs=[
            pl.BlockSpec((halo, d), lambda i, f: (jnp.maximum(i * hb - 1, 0), 0)),
            pl.BlockSpec((tm, d), lambda i, f: (i, 0)),
            pl.BlockSpec((halo, d), lambda i, f: (jnp.minimum((i + 1) * hb, last), 0)),
            pl.BlockSpec((None, 1, d), lambda i, f: (layer, 0, 0)),
            pl.BlockSpec((None, d, tf), lambda i, f: (layer, 0, f)),
            pl.BlockSpec((None, d, tf), lambda i, f: (layer, 0, f + nf)),
            pl.BlockSpec((None, conv_w.shape[1], tf), lambda i, f: (layer, 0, f)),
            pl.BlockSpec((None, 1, tf), lambda i, f: (layer, 0, f)),
            pl.BlockSpec((None, tf, d), lambda i, f: (layer, f, 0)),
            pl.BlockSpec((1, d), lambda i, f: (0, 0)),
        ],
        out_specs=pl.BlockSpec((tm, d), lambda i, f: (i, 0)),
        out_shape=jax.ShapeDtypeStruct((m, d), F32),
        scratch_shapes=[
            pltpu.VMEM((tm + 2 * halo, d), BF16),
            pltpu.VMEM((tm + 2 * halo, tf), F32),
        ],
        compiler_params=_params(("parallel", "arbitrary")),
        name="ffn",
    )(x, x, x, gain, w_in, w_in, conv_w, conv_b, w_out, final_gain)


def kernel(x_prompt, x_sample, norm_mix, norm_ffn, norm_final, ab_w_in, ab_w_out, a_v_gain, a_w_s, a_b_s,
           b_w_g, b_scale, cd_w_in, cd_w_out, c_conv_w, c_conv_b, c_ln_g, c_ln_b, f_w_in, f_conv_w,
           f_conv_b, f_w_out):
    depth, d = norm_mix.shape
    row = lambda v: v.reshape(v.shape[0], 1, v.shape[1])
    ab_w_in_b, ab_w_out_b = ab_w_in.astype(BF16), ab_w_out.astype(BF16)
    cd_w_in_b, cd_w_out_b = cd_w_in.astype(BF16), cd_w_out.astype(BF16)
    f_w_in_b, f_w_out_b = f_w_in.astype(BF16), f_w_out.astype(BF16)
    a_w_s_b, b_w_g_b = a_w_s.astype(BF16), b_w_g.astype(BF16)
    b_rows = jnp.repeat(jnp.swapaxes(a_b_s, 1, 2), GROUP_DIM, axis=2)
    norm_ffn_r, f_conv_b_r = row(norm_ffn), row(f_conv_b)
    a_v_gain_r, b_scale_r = row(a_v_gain), row(b_scale)
    c_conv_b_r, c_ln_g_r, c_ln_b_r = row(c_conv_b), row(c_ln_g), row(c_ln_b)
    c_conv_w_t = jnp.broadcast_to(c_conv_w[:, :, None, :], c_conv_w.shape[:2] + (F32_ROWS, c_conv_w.shape[2]))
    final_gain = norm_final.reshape(1, d)

    def trunk(x):
        bsz, seq, _ = x.shape
        h = x.reshape(bsz * seq, d)
        tables = _dft_tables(seq) if depth > 1 else None
        for l in range(depth):
            i = l // 2
            if l % 2 == 0:
                p = _proj_in(h, norm_mix[l], ab_w_in_b, i, tm=1024, tn=1024)
                y = _mixer_even(p, a_v_gain_r, a_w_s_b, b_rows, b_w_g_b, b_scale_r, i, seq, tm=256)
                h = _proj_out(h, y, 0, y, 1, ab_w_out_b, i, tm=512)
            else:
                p = _proj_in(h, norm_mix[l], cd_w_in_b, i, tm=1024, tn=1024)
                yc = _mixer_c(p, c_conv_w_t, c_conv_b_r, c_ln_g_r, c_ln_b_r, i, seq, tm=256)
                yd = _mixer_d(p, bsz, seq, tables)
                h = _proj_out(h, yc, 0, yd, 0, cd_w_out_b, i, tm=512)
            h = _ffn(h, norm_ffn_r, f_w_in_b, f_conv_w, f_conv_b_r, f_w_out_b, final_gain, l, seq,
                     tm=1024, tf=512, final=(l == depth - 1))
        return h.reshape(bsz, seq, d)

    return trunk(x_prompt), trunk(x_sample)
```

```python
import functools

import numpy as np
import jax
import jax.numpy as jnp
from jax import lax
from jax.experimental import pallas as pl
from jax.experimental.pallas import tpu as pltpu

F32 = jnp.float32
BF16 = jnp.bfloat16
EPS = 1e-6

CHUNK = 128
A_HEADS = 4
POOL_WINDOWS = (2, 4, 8, 16)
GROUP_DIM = 256
DFT_N1 = 128

VMEM_LIMIT = 56 * 1024 * 1024
BF16_ROWS = 16
F32_ROWS = 8


def _rms(x, g):
    return x * lax.rsqrt(jnp.mean(x * x, axis=-1, keepdims=True) + EPS) * g


def _params(sem):
    return pltpu.CompilerParams(dimension_semantics=sem, vmem_limit_bytes=VMEM_LIMIT)


def _halo_specs(tm, m, d, halo):
    hb = tm // halo
    last = m // halo - 1
    return (pl.BlockSpec((halo, d), lambda i: (jnp.maximum(i * hb - 1, 0), 0)),
            pl.BlockSpec((halo, d), lambda i: (jnp.minimum((i + 1) * hb, last), 0)))


def _resident(shape, index_map):
    return pl.BlockSpec(shape, index_map, pipeline_mode=pl.Buffered(1))


def _norm_with_halo(hbuf, xp_ref, x_ref, xn_ref, g, it, tps):
    halo = xp_ref.shape[0]
    tm = x_ref.shape[0]
    hbuf[halo:halo + tm, :] = _rms(x_ref[...], g).astype(BF16)
    hbuf[0:halo, :] = jnp.where(it == 0, 0.0, _rms(xp_ref[...], g)).astype(BF16)
    hbuf[halo + tm:2 * halo + tm, :] = jnp.where(it == tps - 1, 0.0, _rms(xn_ref[...], g)).astype(BF16)


def _even_layer_kernel(xp_ref, x_ref, xn_ref, g_ref, w_ref, vg_ref, ws_ref, bm_ref, wg_ref, sc_ref, wo_ref,
                       o_ref, hbuf, ubuf, vbuf, xbuf, ycat, *, tm, tps, seq):
    it = pl.program_id(0) % tps
    halo = xp_ref.shape[0]
    aw = ubuf.shape[1]
    _norm_with_halo(hbuf, xp_ref, x_ref, xn_ref, g_ref[...], it, tps)
    main = slice(halo, halo + tm)
    ubuf[...] = jnp.dot(hbuf[main, :], w_ref[:, 0:aw], preferred_element_type=F32)
    vbuf[...] = jnp.dot(hbuf[main, :], w_ref[:, aw:2 * aw], preferred_element_type=F32)
    xbuf[...] = jnp.dot(hbuf[...], w_ref[:, 2 * aw:3 * aw], preferred_element_type=F32)

    for c in range(tm // CHUNK):
        rows = slice(c * CHUNK, (c + 1) * CHUNK)
        vn = _rms(jax.nn.gelu(vbuf[rows, :]), vg_ref[...]).astype(BF16)
        for h in range(A_HEADS):
            cols = slice(h * GROUP_DIM, (h + 1) * GROUP_DIM)
            mixed = jnp.dot(ws_ref[h], vn[:, cols], preferred_element_type=F32)
            ycat[rows, cols] = (jax.nn.gelu(ubuf[rows, cols]) * (mixed + bm_ref[:, cols])).astype(BF16)

    pos = it * tm + lax.broadcasted_iota(jnp.int32, (tm, GROUP_DIM), 0)
    for g, w in enumerate(POOL_WINDOWS):
        half = w // 2
        cols = slice(g * GROUP_DIM, (g + 1) * GROUP_DIM)
        s = xbuf[pl.ds(halo - half, tm), cols]
        for dlt in range(-half + 1, half):
            s = s + xbuf[pl.ds(halo + dlt, tm), cols]
        cnt = (jnp.minimum(pos + half, seq) - jnp.maximum(pos - half, 0)).astype(F32)
        pooled = (s / cnt - xbuf[main, cols]).astype(BF16)
        mixed = jnp.dot(pooled, wg_ref[g], preferred_element_type=F32)
        ycat[:, aw + g * GROUP_DIM:aw + (g + 1) * GROUP_DIM] = (mixed * sc_ref[:, cols]).astype(BF16)

    o_ref[...] = x_ref[...] + jnp.dot(ycat[...], wo_ref[...], preferred_element_type=F32)


def _even_layer(x, gain, norm_idx, w_in, v_gain, w_s, b_rows, w_g, scale, w_out, layer, seq, *, tm):
    m, d = x.shape
    aw = w_in.shape[2] // 3
    halo = BF16_ROWS
    prev, nxt = _halo_specs(tm, m, d, halo)
    kern = functools.partial(_even_layer_kernel, tm=tm, tps=seq // tm, seq=seq)
    return pl.pallas_call(
        kern,
        grid=(m // tm,),
        in_specs=[
            prev,
            pl.BlockSpec((tm, d), lambda i: (i, 0)),
            nxt,
            _resident((None, 1, d), lambda i: (norm_idx, 0, 0)),
            _resident((None, d, 3 * aw), lambda i: (layer, 0, 0)),
            _resident((None, 1, aw), lambda i: (layer, 0, 0)),
            _resident((None, A_HEADS, CHUNK, CHUNK), lambda i: (layer, 0, 0, 0)),
            _resident((None, CHUNK, aw), lambda i: (layer, 0, 0)),
            _resident((None, len(POOL_WINDOWS), GROUP_DIM, GROUP_DIM), lambda i: (layer, 0, 0, 0)),
            _resident((None, 1, aw), lambda i: (layer, 0, 0)),
            _resident((None, 2 * aw, d), lambda i: (layer, 0, 0)),
        ],
        out_specs=pl.BlockSpec((tm, d), lambda i: (i, 0)),
        out_shape=jax.ShapeDtypeStruct((m, d), F32),
        scratch_shapes=[
            pltpu.VMEM((tm + 2 * halo, d), BF16),
            pltpu.VMEM((tm, aw), F32),
            pltpu.VMEM((tm, aw), F32),
            pltpu.VMEM((tm + 2 * halo, aw), F32),
            pltpu.VMEM((tm, 2 * aw), BF16),
        ],
        compiler_params=_params(("parallel",)),
        name="even_layer",
    )(x, x, x, gain, w_in, v_gain, w_s, b_rows, w_g, scale, w_out)


def _odd_front_kernel(xp_ref, x_ref, xn_ref, g_ref, w_ref, cw_ref, cb_ref, lg_ref, lb_ref,
                      yc_ref, xd_ref, hbuf, ysh, cbuf, *, tm, tps, cblk):
    it = pl.program_id(0) % tps
    halo = xp_ref.shape[0]
    wdt = yc_ref.shape[1]
    taps = cw_ref.shape[0]
    rows = F32_ROWS
    _norm_with_halo(hbuf, xp_ref, x_ref, xn_ref, g_ref[...], it, tps)
    xd_ref[...] = jnp.dot(hbuf[halo:halo + tm, :], w_ref[:, 2 * wdt:3 * wdt], preferred_element_type=F32)

    span = tm + 2 * halo - rows
    first = halo - taps // 2
    for cb in range(wdt // cblk):
        cols = slice(cb * cblk, (cb + 1) * cblk)
        gcols = slice(wdt + cb * cblk, wdt + (cb + 1) * cblk)
        a = jnp.dot(hbuf[...], w_ref[:, cols], preferred_element_type=F32)
        gate = jnp.dot(hbuf[...], w_ref[:, gcols], preferred_element_type=F32)
        buf = ysh.at[cb % ysh.shape[0]]
        buf[0, :, :] = a * jax.nn.sigmoid(gate)
        for r in range(1, rows):
            buf[r, 0:span, :] = buf[0, pl.ds(r, span), :]
        for blk in range(tm // rows):
            acc = None
            for k in range(taps):
                q, r = divmod(first + k, rows)
                term = cw_ref[k, :, cols] * buf[r, pl.ds((blk + q) * rows, rows), :]
                acc = term if acc is None else acc + term
            cbuf[blk * rows:(blk + 1) * rows, cols] = acc

    grp = BF16_ROWS
    for blk in range(tm // grp):
        rsl = slice(blk * grp, (blk + 1) * grp)
        y = cbuf[rsl, :] + cb_ref[...]
        yc = y - jnp.mean(y, axis=-1, keepdims=True)
        ln = yc * lax.rsqrt(jnp.mean(yc * yc, axis=-1, keepdims=True) + EPS) * lg_ref[...] + lb_ref[...]
        yc_ref[rsl, :] = (ln * jax.nn.sigmoid(ln)).astype(BF16)


def _odd_front(x, gain, norm_idx, w_in, conv_w, conv_b, ln_g, ln_b, layer, seq, *, tm, cblk=256):
    m, d = x.shape
    wdt = w_in.shape[2] // 3
    taps = conv_w.shape[1]
    halo = BF16_ROWS
    assert taps // 2 <= halo
    prev, nxt = _halo_specs(tm, m, d, halo)
    kern = functools.partial(_odd_front_kernel, tm=tm, tps=seq // tm, cblk=cblk)
    vec = _resident((None, 1, wdt), lambda i: (layer, 0, 0))
    return pl.pallas_call(
        kern,
        grid=(m // tm,),
        in_specs=[
            prev,
            pl.BlockSpec((tm, d), lambda i: (i, 0)),
            nxt,
            _resident((None, 1, d), lambda i: (norm_idx, 0, 0)),
            _resident((None, d, 3 * wdt), lambda i: (layer, 0, 0)),
            _resident((None, taps, F32_ROWS, wdt), lambda i: (layer, 0, 0, 0)),
            vec, vec, vec,
        ],
        out_specs=[pl.BlockSpec((tm, wdt), lambda i: (i, 0)), pl.BlockSpec((tm, wdt), lambda i: (i, 0))],
        out_shape=[jax.ShapeDtypeStruct((m, wdt), BF16), jax.ShapeDtypeStruct((m, wdt), F32)],
        scratch_shapes=[
            pltpu.VMEM((tm + 2 * halo, d), BF16),
            pltpu.VMEM((2, F32_ROWS, tm + 2 * halo, cblk), F32),
            pltpu.VMEM((tm, wdt), F32),
        ],
        compiler_params=_params(("parallel",)),
        name="odd_front",
    )(x, x, x, gain, w_in, conv_w, conv_b, ln_g, ln_b)


def _proj_out_kernel(x_ref, y1_ref, y2_ref, w1_ref, w2_ref, o_ref):
    acc = jnp.dot(y1_ref[...].astype(BF16), w1_ref[...], preferred_element_type=F32)
    acc += jnp.dot(y2_ref[...].astype(BF16), w2_ref[...], preferred_element_type=F32)
    o_ref[...] = x_ref[...] + acc


def _proj_out(x, y1, y2, w, layer, *, tm):
    m, d = x.shape
    kh = w.shape[1] // 2
    return pl.pallas_call(
        _proj_out_kernel,
        grid=(m // tm,),
        in_specs=[
            pl.BlockSpec((tm, d), lambda i: (i, 0)),
            pl.BlockSpec((tm, kh), lambda i: (i, 0)),
            pl.BlockSpec((tm, kh), lambda i: (i, 0)),
            _resident((None, kh, d), lambda i: (layer, 0, 0)),
            _resident((None, kh, d), lambda i: (layer, 1, 0)),
        ],
        out_specs=pl.BlockSpec((tm, d), lambda i: (i, 0)),
        out_shape=jax.ShapeDtypeStruct((m, d), F32),
        compiler_params=_params(("parallel",)),
        name="proj_out",
    )(x, y1, y2, w, w)


def _dft_tables(seq):
    n1 = DFT_N1
    n2 = seq // n1
    c = np.arange(GROUP_DIM)
    ang = 2.0 * np.pi * ((c[:, None] * c[None, :]) % GROUP_DIM) / GROUP_DIM
    fc = jnp.asarray(np.concatenate([np.cos(ang), -np.sin(ang)], axis=1), BF16)

    shape = (n2, n1, n1)
    r = lax.broadcasted_iota(jnp.int32, shape, 0)
    k1 = lax.broadcasted_iota(jnp.int32, shape, 1)
    j = lax.broadcasted_iota(jnp.int32, shape, 2)
    idx = (k1 * (n2 * j + r)) % seq
    a1 = idx.astype(F32) * (2.0 * np.pi / seq)
    gr, gi = jnp.cos(a1), -jnp.sin(a1)
    m1 = jnp.concatenate([jnp.concatenate([gr, -gi], axis=2),
                          jnp.concatenate([gi, gr], axis=2)], axis=1).astype(BF16)

    k2 = np.arange(n2)
    a2 = 2.0 * np.pi * ((k2[:, None] * k2[None, :]) % n2) / n2
    norm = 1.0 / np.sqrt(float(seq) * GROUP_DIM)
    wc = jnp.asarray(np.cos(a2) * norm, BF16)
    ws = jnp.asarray(np.sin(a2) * norm, BF16)
    return fc, m1, wc, ws


def _dft1_kernel(x_ref, fc_ref, m1_ref, o_ref, zbuf):
    n1, rt, wdt = x_ref.shape
    xt = pltpu.einshape("jrc->rjc", x_ref[...])
    for rr in range(rt):
        x = xt[rr].astype(BF16)
        for g in range(x_ref.shape[2] // GROUP_DIM):
            cols = slice(g * GROUP_DIM, (g + 1) * GROUP_DIM)
            zz = jnp.dot(x[:, cols], fc_ref[...], preferred_element_type=F32)
            zbuf[0:n1, cols] = zz[:, :GROUP_DIM].astype(BF16)
            zbuf[n1:2 * n1, cols] = zz[:, GROUP_DIM:].astype(BF16)
        y = jnp.dot(m1_ref[rr], zbuf[...], preferred_element_type=F32)
        o_ref[0, rr] = y[:n1]
        o_ref[1, rr] = y[n1:]


def _dft2_kernel(wc_ref, ws_ref, y_ref, o_ref):
    rt = y_ref.shape[2]
    yr_t = pltpu.einshape("rkc->krc", y_ref[0])
    yi_t = pltpu.einshape("rkc->krc", y_ref[1])
    res = [jnp.dot(wc_ref[...], yr_t[kk].astype(BF16), preferred_element_type=F32)
           + jnp.dot(ws_ref[...], yi_t[kk].astype(BF16), preferred_element_type=F32) for kk in range(rt)]
    o_ref[...] = pltpu.einshape("krc->rkc", jnp.stack(res, axis=0))


def _mixer_d(xd, bsz, seq, tables):
    fc, m1, wc, ws = tables
    m, wdt = xd.shape
    n1 = DFT_N1
    n2 = seq // n1
    rt = F32_ROWS
    pv = xd.reshape(bsz, n1, n2 // rt, rt, wdt)
    yp = pl.pallas_call(
        _dft1_kernel,
        grid=(bsz, n2 // rt),
        in_specs=[
            pl.BlockSpec((None, n1, None, rt, wdt), lambda b, r: (b, 0, r, 0, 0)),
            pl.BlockSpec((GROUP_DIM, 2 * GROUP_DIM), lambda b, r: (0, 0)),
            pl.BlockSpec((rt, 2 * n1, 2 * n1), lambda b, r: (r, 0, 0)),
        ],
        out_specs=pl.BlockSpec((None, 2, rt, n1, wdt), lambda b, r: (b, 0, r, 0, 0)),
        out_shape=jax.ShapeDtypeStruct((bsz, 2, n2, n1, wdt), F32),
        scratch_shapes=[pltpu.VMEM((2 * n1, wdt), BF16)],
        compiler_params=_params(("parallel", "parallel")),
        name="dft_stage1",
    )(pv, fc, m1)
    out = pl.pallas_call(
        _dft2_kernel,
        grid=(bsz, n1 // rt),
        in_specs=[
            pl.BlockSpec((n2, n2), lambda b, c: (0, 0)),
            pl.BlockSpec((n2, n2), lambda b, c: (0, 0)),
            pl.BlockSpec((None, 2, n2, rt, wdt), lambda b, c: (b, 0, 0, c, 0)),
        ],
        out_specs=pl.BlockSpec((None, n2, rt, wdt), lambda b, c: (b, 0, c, 0)),
        out_shape=jax.ShapeDtypeStruct((bsz, n2, n1, wdt), F32),
        compiler_params=_params(("parallel", "parallel")),
        name="dft_stage2",
    )(wc, ws, yp)
    return out.reshape(m, wdt)


def _ffn_kernel(xp_ref, x_ref, xn_ref, g_ref, wa_ref, wg_ref, cw_ref, cb_ref, wo_ref, gf_ref,
                o_ref, hbuf, abuf, *, tm, tps, final):
    it = pl.program_id(0) % tps
    f = pl.program_id(1)
    halo = xp_ref.shape[0]

    @pl.when(f == 0)
    def _():
        g = g_ref[...]
        x = x_ref[...]
        hbuf[halo:halo + tm, :] = _rms(x, g).astype(BF16)
        hbuf[0:halo, :] = jnp.where(it == 0, 0.0, _rms(xp_ref[...], g)).astype(BF16)
        hbuf[halo + tm:2 * halo + tm, :] = jnp.where(it == tps - 1, 0.0, _rms(xn_ref[...], g)).astype(BF16)
        o_ref[...] = x

    abuf[...] = jnp.dot(hbuf[...], wa_ref[...], preferred_element_type=F32)
    gate = jnp.dot(hbuf[halo:halo + tm, :], wg_ref[...], preferred_element_type=F32)
    a = (cw_ref[0:1, :] * abuf[pl.ds(halo - 1, tm), :]
         + cw_ref[1:2, :] * abuf[pl.ds(halo, tm), :]
         + cw_ref[2:3, :] * abuf[pl.ds(halo + 1, tm), :]) + cb_ref[...]
    act = (jax.nn.gelu(a) * gate).astype(BF16)
    o_ref[...] += jnp.dot(act, wo_ref[...], preferred_element_type=F32)

    if final:
        @pl.when(f == pl.num_programs(1) - 1)
        def _():
            o_ref[...] = _rms(o_ref[...], gf_ref[...])


def _ffn(x, gain, w_in, conv_w, conv_b, w_out, final_gain, layer, seq, *, tm, tf, final):
    m, d = x.shape
    dff = w_out.shape[1]
    nf = dff // tf
    tps = seq // tm
    halo = BF16_ROWS
    hb = tm // halo
    last = m // halo - 1
    kern = functools.partial(_ffn_kernel, tm=tm, tps=tps, final=final)
    return pl.pallas_call(
        kern,
        grid=(m // tm, nf),
        in_specs=[
            pl.BlockSpec((halo, d), lambda i, f: (jnp.maximum(i * hb - 1, 0), 0)),
            pl.BlockSpec((tm, d), lambda i, f: (i, 0)),
            pl.BlockSpec((halo, d), lambda i, f: (jnp.minimum((i + 1) * hb, last), 0)),
            pl.BlockSpec((None, 1, d), lambda i, f: (layer, 0, 0)),
            pl.BlockSpec((None, d, tf), lambda i, f: (layer, 0, f)),
            pl.BlockSpec((None, d, tf), lambda i, f: (layer, 0, f + nf)),
            pl.BlockSpec((None, conv_w.shape[1], tf), lambda i, f: (layer, 0, f)),
            pl.BlockSpec((None, 1, tf), lambda i, f: (layer, 0, f)),
            pl.BlockSpec((None, tf, d), lambda i, f: (layer, f, 0)),
            pl.BlockSpec((1, d), lambda i, f: (0, 0)),
        ],
        out_specs=pl.BlockSpec((tm, d), lambda i, f: (i, 0)),
        out_shape=jax.ShapeDtypeStruct((m, d), F32),
        scratch_shapes=[
            pltpu.VMEM((tm + 2 * halo, d), BF16),
            pltpu.VMEM((tm + 2 * halo, tf), F32),
        ],
        compiler_params=_params(("parallel", "arbitrary")),
        name="ffn",
    )(x, x, x, gain, w_in, w_in, conv_w, conv_b, w_out, final_gain)


def kernel(x_prompt, x_sample, norm_mix, norm_ffn, norm_final, ab_w_in, ab_w_out, a_v_gain, a_w_s, a_b_s,
           b_w_g, b_scale, cd_w_in, cd_w_out, c_conv_w, c_conv_b, c_ln_g, c_ln_b, f_w_in, f_conv_w,
           f_conv_b, f_w_out):
    depth, d = norm_mix.shape
    row = lambda v: v.reshape(v.shape[0], 1, v.shape[1])
    ab_w_in_b, ab_w_out_b = ab_w_in.astype(BF16), ab_w_out.astype(BF16)
    cd_w_in_b, cd_w_out_b = cd_w_in.astype(BF16), cd_w_out.astype(BF16)
    f_w_in_b, f_w_out_b = f_w_in.astype(BF16), f_w_out.astype(BF16)
    a_w_s_b, b_w_g_b = a_w_s.astype(BF16), b_w_g.astype(BF16)
    b_rows = jnp.repeat(jnp.swapaxes(a_b_s, 1, 2), GROUP_DIM, axis=2)
    norm_mix_r, norm_ffn_r, f_conv_b_r = row(norm_mix), row(norm_ffn), row(f_conv_b)
    a_v_gain_r, b_scale_r = row(a_v_gain), row(b_scale)
    c_conv_b_r, c_ln_g_r, c_ln_b_r = row(c_conv_b), row(c_ln_g), row(c_ln_b)
    c_conv_w_t = jnp.broadcast_to(c_conv_w[:, :, None, :], c_conv_w.shape[:2] + (F32_ROWS, c_conv_w.shape[2]))
    final_gain = norm_final.reshape(1, d)

    def trunk(x):
        bsz, seq, _ = x.shape
        h = x.reshape(bsz * seq, d)
        tables = _dft_tables(seq) if depth > 1 else None
        for l in range(depth):
            i = l // 2
            if l % 2 == 0:
                h = _even_layer(h, norm_mix_r, l, ab_w_in_b, a_v_gain_r, a_w_s_b, b_rows, b_w_g_b, b_scale_r,
                                ab_w_out_b, i, seq, tm=512)
            else:
                yc, xd = _odd_front(h, norm_mix_r, l, cd_w_in_b, c_conv_w_t, c_conv_b_r, c_ln_g_r, c_ln_b_r,
                                    i, seq, tm=512)
                yd = _mixer_d(xd, bsz, seq, tables)
                h = _proj_out(h, yc, yd, cd_w_out_b, i, tm=512)
            h = _ffn(h, norm_ffn_r, f_w_in_b, f_conv_w, f_conv_b_r, f_w_out_b, final_gain, l, seq,
                     tm=1024, tf=512, final=(l == depth - 1))
        return h.reshape(bsz, seq, d)

    return trunk(x_prompt), trunk(x_sample)
```

```python
import functools

import numpy as np
import jax
import jax.numpy as jnp
from jax import lax
from jax.experimental import pallas as pl
from jax.experimental.pallas import tpu as pltpu

F32 = jnp.float32
BF16 = jnp.bfloat16
EPS = 1e-6

CHUNK = 128
A_HEADS = 4
POOL_WINDOWS = (2, 4, 8, 16)
GROUP_DIM = 256
DFT_N1 = 128
CONV_ROW_BLOCKS = 4
COL_BLOCK = 256

VMEM_LIMIT = 56 * 1024 * 1024
BF16_ROWS = 16
F32_ROWS = 8

MIXER_TM = 512
FFN_TM = 1024
FFN_TF = 512


def _rms(x, g):
    return x * lax.rsqrt(jnp.mean(x * x, axis=-1, keepdims=True) + EPS) * g


def _params(sem):
    return pltpu.CompilerParams(dimension_semantics=sem, vmem_limit_bytes=VMEM_LIMIT)


def _halo_specs(tm, m, d, halo):
    hb = tm // halo
    last = m // halo - 1
    return (pl.BlockSpec((halo, d), lambda i: (jnp.maximum(i * hb - 1, 0), 0)),
            pl.BlockSpec((halo, d), lambda i: (jnp.minimum((i + 1) * hb, last), 0)))


def _resident(shape, index_map):
    return pl.BlockSpec(shape, index_map, pipeline_mode=pl.Buffered(1))


def _norm_with_halo(hbuf, xp_ref, x_ref, xn_ref, g, it, tps):
    halo = xp_ref.shape[0]
    tm = x_ref.shape[0]
    hbuf[halo:halo + tm, :] = _rms(x_ref[...], g).astype(BF16)
    hbuf[0:halo, :] = jnp.where(it == 0, 0.0, _rms(xp_ref[...], g)).astype(BF16)
    hbuf[halo + tm:2 * halo + tm, :] = jnp.where(it == tps - 1, 0.0, _rms(xn_ref[...], g)).astype(BF16)


def _even_layer_kernel(xp_ref, x_ref, xn_ref, g_ref, w_ref, vg_ref, ws_ref, bm_ref, wg_ref, sc_ref, wo_ref,
                       o_ref, hbuf, ubuf, vbuf, xbuf, ycat, *, tm, tps, seq):
    it = pl.program_id(0) % tps
    halo = xp_ref.shape[0]
    aw = ubuf.shape[1]
    _norm_with_halo(hbuf, xp_ref, x_ref, xn_ref, g_ref[...], it, tps)
    main = slice(halo, halo + tm)
    vbuf[...] = jnp.dot(hbuf[main, :], w_ref[:, aw:2 * aw], preferred_element_type=F32)
    xbuf[...] = jnp.dot(hbuf[...], w_ref[:, 2 * aw:3 * aw], preferred_element_type=F32)
    ubuf[...] = jnp.dot(hbuf[main, :], w_ref[:, 0:aw], preferred_element_type=F32)

    for c in range(tm // CHUNK):
        rows = slice(c * CHUNK, (c + 1) * CHUNK)
        vn = _rms(jax.nn.gelu(vbuf[rows, :]), vg_ref[...]).astype(BF16)
        for h in range(A_HEADS):
            cols = slice(h * GROUP_DIM, (h + 1) * GROUP_DIM)
            mixed = jnp.dot(ws_ref[h], vn[:, cols], preferred_element_type=F32)
            ycat[rows, cols] = (jax.nn.gelu(ubuf[rows, cols]) * (mixed + bm_ref[:, cols])).astype(BF16)

    pos = it * tm + lax.broadcasted_iota(jnp.int32, (tm, GROUP_DIM), 0)
    for g, w in enumerate(POOL_WINDOWS):
        half = w // 2
        cols = slice(g * GROUP_DIM, (g + 1) * GROUP_DIM)
        s = xbuf[pl.ds(halo - half, tm), cols]
        for dlt in range(-half + 1, half):
            s = s + xbuf[pl.ds(halo + dlt, tm), cols]
        cnt = (jnp.minimum(pos + half, seq) - jnp.maximum(pos - half, 0)).astype(F32)
        pooled = (s / cnt - xbuf[main, cols]).astype(BF16)
        mixed = jnp.dot(pooled, wg_ref[g], preferred_element_type=F32)
        ycat[:, aw + g * GROUP_DIM:aw + (g + 1) * GROUP_DIM] = (mixed * sc_ref[:, cols]).astype(BF16)

    o_ref[...] = x_ref[...] + jnp.dot(ycat[...], wo_ref[...], preferred_element_type=F32)


def _even_layer(x, gain, norm_idx, w_in, v_gain, w_s, b_rows, w_g, scale, w_out, layer, seq, *, tm):
    m, d = x.shape
    aw = w_in.shape[2] // 3
    halo = BF16_ROWS
    assert seq % tm == 0 and m % seq == 0 and tm % CHUNK == 0 and max(POOL_WINDOWS) // 2 <= halo
    assert aw == A_HEADS * GROUP_DIM == len(POOL_WINDOWS) * GROUP_DIM
    prev, nxt = _halo_specs(tm, m, d, halo)
    kern = functools.partial(_even_layer_kernel, tm=tm, tps=seq // tm, seq=seq)
    return pl.pallas_call(
        kern,
        grid=(m // tm,),
        in_specs=[
            prev,
            pl.BlockSpec((tm, d), lambda i: (i, 0)),
            nxt,
            _resident((None, 1, d), lambda i: (norm_idx, 0, 0)),
            _resident((None, d, 3 * aw), lambda i: (layer, 0, 0)),
            _resident((None, 1, aw), lambda i: (layer, 0, 0)),
            _resident((None, A_HEADS, CHUNK, CHUNK), lambda i: (layer, 0, 0, 0)),
            _resident((None, CHUNK, aw), lambda i: (layer, 0, 0)),
            _resident((None, len(POOL_WINDOWS), GROUP_DIM, GROUP_DIM), lambda i: (layer, 0, 0, 0)),
            _resident((None, 1, aw), lambda i: (layer, 0, 0)),
            _resident((None, 2 * aw, d), lambda i: (layer, 0, 0)),
        ],
        out_specs=pl.BlockSpec((tm, d), lambda i: (i, 0)),
        out_shape=jax.ShapeDtypeStruct((m, d), F32),
        scratch_shapes=[
            pltpu.VMEM((tm + 2 * halo, d), BF16),
            pltpu.VMEM((tm, aw), F32),
            pltpu.VMEM((tm, aw), F32),
            pltpu.VMEM((tm + 2 * halo, aw), F32),
            pltpu.VMEM((tm, 2 * aw), BF16),
        ],
        compiler_params=_params(("parallel",)),
        name="even_layer",
    )(x, x, x, gain, w_in, v_gain, w_s, b_rows, w_g, scale, w_out)


def _odd_front_kernel(xp_ref, x_ref, xn_ref, g_ref, w_ref, cw_ref, cb_ref, lg_ref, lb_ref,
                      yc_ref, xd_ref, hbuf, ysh, cbuf, *, tm, tps, cblk):
    it = pl.program_id(0) % tps
    halo = xp_ref.shape[0]
    wdt = yc_ref.shape[1]
    taps = cw_ref.shape[0]
    rows = F32_ROWS
    _norm_with_halo(hbuf, xp_ref, x_ref, xn_ref, g_ref[...], it, tps)
    xd_ref[...] = jnp.dot(hbuf[halo:halo + tm, :], w_ref[:, 2 * wdt:3 * wdt], preferred_element_type=F32)

    span = tm + 2 * halo - rows
    first = halo - taps // 2
    for cb in range(wdt // cblk):
        cols = slice(cb * cblk, (cb + 1) * cblk)
        gcols = slice(wdt + cb * cblk, wdt + (cb + 1) * cblk)
        a = jnp.dot(hbuf[...], w_ref[:, cols], preferred_element_type=F32)
        gate = jnp.dot(hbuf[...], w_ref[:, gcols], preferred_element_type=F32)
        buf = ysh.at[cb % ysh.shape[0]]
        buf[0, :, :] = a * jax.nn.sigmoid(gate)
        for r in range(1, rows):
            buf[r, 0:span, :] = buf[0, pl.ds(r, span), :]
        for blk0 in range(0, tm // rows, CONV_ROW_BLOCKS):
            accs = [None] * CONV_ROW_BLOCKS
            for k in range(taps):
                q, r = divmod(first + k, rows)
                wk = cw_ref[k, :, cols]
                for j in range(CONV_ROW_BLOCKS):
                    term = wk * buf[r, pl.ds((blk0 + j + q) * rows, rows), :]
                    accs[j] = term if accs[j] is None else accs[j] + term
            for j in range(CONV_ROW_BLOCKS):
                cbuf[(blk0 + j) * rows:(blk0 + j + 1) * rows, cols] = accs[j]

    grp = BF16_ROWS
    for blk in range(tm // grp):
        rsl = slice(blk * grp, (blk + 1) * grp)
        y = cbuf[rsl, :] + cb_ref[...]
        yc = y - jnp.mean(y, axis=-1, keepdims=True)
        ln = yc * lax.rsqrt(jnp.mean(yc * yc, axis=-1, keepdims=True) + EPS) * lg_ref[...] + lb_ref[...]
        yc_ref[rsl, :] = (ln * jax.nn.sigmoid(ln)).astype(BF16)


def _odd_front(x, gain, norm_idx, w_in, conv_w, conv_b, ln_g, ln_b, layer, seq, *, tm, cblk=COL_BLOCK):
    m, d = x.shape
    wdt = w_in.shape[2] // 3
    taps = conv_w.shape[1]
    halo = BF16_ROWS
    assert taps // 2 <= halo and seq % tm == 0 and m % seq == 0 and wdt % cblk == 0
    assert (tm // F32_ROWS) % CONV_ROW_BLOCKS == 0
    prev, nxt = _halo_specs(tm, m, d, halo)
    kern = functools.partial(_odd_front_kernel, tm=tm, tps=seq // tm, cblk=cblk)
    vec = _resident((None, 1, wdt), lambda i: (layer, 0, 0))
    return pl.pallas_call(
        kern,
        grid=(m // tm,),
        in_specs=[
            prev,
            pl.BlockSpec((tm, d), lambda i: (i, 0)),
            nxt,
            _resident((None, 1, d), lambda i: (norm_idx, 0, 0)),
            _resident((None, d, 3 * wdt), lambda i: (layer, 0, 0)),
            _resident((None, taps, F32_ROWS, wdt), lambda i: (layer, 0, 0, 0)),
            vec, vec, vec,
        ],
        out_specs=[pl.BlockSpec((tm, wdt), lambda i: (i, 0)), pl.BlockSpec((tm, wdt), lambda i: (i, 0))],
        out_shape=[jax.ShapeDtypeStruct((m, wdt), BF16), jax.ShapeDtypeStruct((m, wdt), F32)],
        scratch_shapes=[
            pltpu.VMEM((tm + 2 * halo, d), BF16),
            pltpu.VMEM((2, F32_ROWS, tm + 2 * halo, cblk), F32),
            pltpu.VMEM((tm, wdt), F32),
        ],
        compiler_params=_params(("parallel",)),
        name="odd_front",
    )(x, x, x, gain, w_in, conv_w, conv_b, ln_g, ln_b)


def _proj_out_kernel(x_ref, y1_ref, y2_ref, w1_ref, w2_ref, o_ref):
    acc = jnp.dot(y1_ref[...].astype(BF16), w1_ref[...], preferred_element_type=F32)
    acc += jnp.dot(y2_ref[...].astype(BF16), w2_ref[...], preferred_element_type=F32)
    o_ref[...] = x_ref[...] + acc


def _proj_out(x, y1, y2, w, layer, *, tm):
    m, d = x.shape
    kh = w.shape[1] // 2
    return pl.pallas_call(
        _proj_out_kernel,
        grid=(m // tm,),
        in_specs=[
            pl.BlockSpec((tm, d), lambda i: (i, 0)),
            pl.BlockSpec((tm, kh), lambda i: (i, 0)),
            pl.BlockSpec((tm, kh), lambda i: (i, 0)),
            _resident((None, kh, d), lambda i: (layer, 0, 0)),
            _resident((None, kh, d), lambda i: (layer, 1, 0)),
        ],
        out_specs=pl.BlockSpec((tm, d), lambda i: (i, 0)),
        out_shape=jax.ShapeDtypeStruct((m, d), F32),
        compiler_params=_params(("parallel",)),
        name="proj_out",
    )(x, y1, y2, w, w)


def _dft_tables(seq):
    n1 = DFT_N1
    n2 = seq // n1
    c = np.arange(GROUP_DIM)
    ang = 2.0 * np.pi * ((c[:, None] * c[None, :]) % GROUP_DIM) / GROUP_DIM
    fc = jnp.asarray(np.concatenate([np.cos(ang), -np.sin(ang)], axis=1), BF16)

    shape = (n2, n1, n1)
    r = lax.broadcasted_iota(jnp.int32, shape, 0)
    k1 = lax.broadcasted_iota(jnp.int32, shape, 1)
    j = lax.broadcasted_iota(jnp.int32, shape, 2)
    idx = (k1 * (n2 * j + r)) % seq
    a1 = idx.astype(F32) * (2.0 * np.pi / seq)
    gr, gi = jnp.cos(a1), -jnp.sin(a1)
    m1 = jnp.concatenate([jnp.concatenate([gr, -gi], axis=2),
                          jnp.concatenate([gi, gr], axis=2)], axis=1).astype(BF16)

    k2 = np.arange(n2)
    a2 = 2.0 * np.pi * ((k2[:, None] * k2[None, :]) % n2) / n2
    norm = 1.0 / np.sqrt(float(seq) * GROUP_DIM)
    wc = jnp.asarray(np.cos(a2) * norm, BF16)
    ws = jnp.asarray(np.sin(a2) * norm, BF16)
    return fc, m1, wc, ws


def _dft1_kernel(x_ref, fc_ref, m1_ref, o_ref, zbuf):
    n1, rt, wdt = x_ref.shape
    xt = pltpu.einshape("jrc->rjc", x_ref[...])
    for rr in range(rt):
        x = xt[rr].astype(BF16)
        for g in range(x_ref.shape[2] // GROUP_DIM):
            cols = slice(g * GROUP_DIM, (g + 1) * GROUP_DIM)
            zz = jnp.dot(x[:, cols], fc_ref[...], preferred_element_type=F32)
            zbuf[0:n1, cols] = zz[:, :GROUP_DIM].astype(BF16)
            zbuf[n1:2 * n1, cols] = zz[:, GROUP_DIM:].astype(BF16)
        y = jnp.dot(m1_ref[rr], zbuf[...], preferred_element_type=F32)
        o_ref[rr] = _pack_bf16_pair(y[:n1], y[n1:])


def _pack_bf16_pair(hi, lo):
    hi_bits = lax.bitcast_convert_type(hi.astype(BF16).astype(F32), jnp.uint32)
    lo_bits = lax.bitcast_convert_type(lo.astype(BF16).astype(F32), jnp.uint32)
    return hi_bits | (lo_bits >> 16)


def _unpack_bf16_pair(word):
    hi = lax.bitcast_convert_type(word & jnp.uint32(0xFFFF0000), F32)
    lo = lax.bitcast_convert_type(word << 16, F32)
    return hi.astype(BF16), lo.astype(BF16)


def _dft2_kernel(wc_ref, ws_ref, y_ref, o_ref):
    rt = y_ref.shape[1]
    y_t = pltpu.einshape("rkc->krc", y_ref[...])
    res = []
    for kk in range(rt):
        yr, yi = _unpack_bf16_pair(y_t[kk])
        res.append(jnp.dot(wc_ref[...], yr, preferred_element_type=F32)
                   + jnp.dot(ws_ref[...], yi, preferred_element_type=F32))
    o_ref[...] = pltpu.einshape("krc->rkc", jnp.stack(res, axis=0))


def _mixer_d(xd, bsz, seq, tables):
    fc, m1, wc, ws = tables
    m, wdt = xd.shape
    n1 = DFT_N1
    n2 = seq // n1
    rt = F32_ROWS
    assert seq == n1 * n2 and n2 % rt == 0 and wdt % GROUP_DIM == 0
    pv = xd.reshape(bsz, n1, n2 // rt, rt, wdt)
    yp = pl.pallas_call(
        _dft1_kernel,
        grid=(bsz, n2 // rt),
        in_specs=[
            pl.BlockSpec((None, n1, None, rt, wdt), lambda b, r: (b, 0, r, 0, 0)),
            pl.BlockSpec((GROUP_DIM, 2 * GROUP_DIM), lambda b, r: (0, 0)),
            pl.BlockSpec((rt, 2 * n1, 2 * n1), lambda b, r: (r, 0, 0)),
        ],
        out_specs=pl.BlockSpec((None, rt, n1, wdt), lambda b, r: (b, r, 0, 0)),
        out_shape=jax.ShapeDtypeStruct((bsz, n2, n1, wdt), jnp.uint32),
        scratch_shapes=[pltpu.VMEM((2 * n1, wdt), BF16)],
        compiler_params=_params(("parallel", "parallel")),
        name="dft_stage1",
    )(pv, fc, m1)
    out = pl.pallas_call(
        _dft2_kernel,
        grid=(bsz, n1 // rt),
        in_specs=[
            pl.BlockSpec((n2, n2), lambda b, c: (0, 0)),
            pl.BlockSpec((n2, n2), lambda b, c: (0, 0)),
            pl.BlockSpec((None, n2, rt, wdt), lambda b, c: (b, 0, c, 0)),
        ],
        out_specs=pl.BlockSpec((None, n2, rt, wdt), lambda b, c: (b, 0, c, 0)),
        out_shape=jax.ShapeDtypeStruct((bsz, n2, n1, wdt), F32),
        compiler_params=_params(("parallel", "parallel")),
        name="dft_stage2",
    )(wc, ws, yp)
    return out.reshape(m, wdt)


def _ffn_kernel(xp_ref, x_ref, xn_ref, g_ref, wa_ref, wg_ref, cw_ref, cb_ref, wo_ref, gf_ref,
                o_ref, hbuf, abuf, *, tm, tps, final):
    it = pl.program_id(0) % tps
    f = pl.program_id(1)
    halo = xp_ref.shape[0]

    @pl.when(f == 0)
    def _():
        g = g_ref[...]
        x = x_ref[...]
        hbuf[halo:halo + tm, :] = _rms(x, g).astype(BF16)
        hbuf[0:halo, :] = jnp.where(it == 0, 0.0, _rms(xp_ref[...], g)).astype(BF16)
        hbuf[halo + tm:2 * halo + tm, :] = jnp.where(it == tps - 1, 0.0, _rms(xn_ref[...], g)).astype(BF16)
        o_ref[...] = x

    abuf[...] = jnp.dot(hbuf[...], wa_ref[...], preferred_element_type=F32)
    gate = jnp.dot(hbuf[halo:halo + tm, :], wg_ref[...], preferred_element_type=F32)
    a = (cw_ref[0:1, :] * abuf[pl.ds(halo - 1, tm), :]
         + cw_ref[1:2, :] * abuf[pl.ds(halo, tm), :]
         + cw_ref[2:3, :] * abuf[pl.ds(halo + 1, tm), :]) + cb_ref[...]
    act = (jax.nn.gelu(a) * gate).astype(BF16)
    o_ref[...] += jnp.dot(act, wo_ref[...], preferred_element_type=F32)

    if final:
        @pl.when(f == pl.num_programs(1) - 1)
        def _():
            o_ref[...] = _rms(o_ref[...], gf_ref[...])


def _ffn(x, gain, w_in, conv_w, conv_b, w_out, final_gain, layer, seq, *, tm, tf, final):
    m, d = x.shape
    dff = w_out.shape[1]
    nf = dff // tf
    tps = seq // tm
    halo = BF16_ROWS
    assert seq % tm == 0 and m % seq == 0 and dff % tf == 0 and w_in.shape[2] == 2 * dff
    hb = tm // halo
    last = m // halo - 1
    kern = functools.partial(_ffn_kernel, tm=tm, tps=tps, final=final)
    return pl.pallas_call(
        kern,
        grid=(m // tm, nf),
        in_specs=[
            pl.BlockSpec((halo, d), lambda i, f: (jnp.maximum(i * hb - 1, 0), 0)),
            pl.BlockSpec((tm, d), lambda i, f: (i, 0)),
            pl.BlockSpec((halo, d), lambda i, f: (jnp.minimum((i + 1) * hb, last), 0)),
            pl.BlockSpec((None, 1, d), lambda i, f: (layer, 0, 0)),
            pl.BlockSpec((None, d, tf), lambda i, f: (layer, 0, f)),
            pl.BlockSpec((None, d, tf), lambda i, f: (layer, 0, f + nf)),
            pl.BlockSpec((None, conv_w.shape[1], tf), lambda i, f: (layer, 0, f)),
            pl.BlockSpec((None, 1, tf), lambda i, f: (layer, 0, f)),
            pl.BlockSpec((None, tf, d), lambda i, f: (layer, f, 0)),
            pl.BlockSpec((1, d), lambda i, f: (0, 0)),
        ],
        out_specs=pl.BlockSpec((tm, d), lambda i, f: (i, 0)),
        out_shape=jax.ShapeDtypeStruct((m, d), F32),
        scratch_shapes=[
            pltpu.VMEM((tm + 2 * halo, d), BF16),
            pltpu.VMEM((tm + 2 * halo, tf), F32),
        ],
        compiler_params=_params(("parallel", "arbitrary")),
        name="ffn",
    )(x, x, x, gain, w_in, w_in, conv_w, conv_b, w_out, final_gain)


def kernel(x_prompt, x_sample, norm_mix, norm_ffn, norm_final, ab_w_in, ab_w_out, a_v_gain, a_w_s, a_b_s,
           b_w_g, b_scale, cd_w_in, cd_w_out, c_conv_w, c_conv_b, c_ln_g, c_ln_b, f_w_in, f_conv_w,
           f_conv_b, f_w_out):
    depth, d = norm_mix.shape
    row = lambda v: v.reshape(v.shape[0], 1, v.shape[1])
    ab_w_in_b, ab_w_out_b = ab_w_in.astype(BF16), ab_w_out.astype(BF16)
    cd_w_in_b, cd_w_out_b = cd_w_in.astype(BF16), cd_w_out.astype(BF16)
    f_w_in_b, f_w_out_b = f_w_in.astype(BF16), f_w_out.astype(BF16)
    a_w_s_b, b_w_g_b = a_w_s.astype(BF16), b_w_g.astype(BF16)
    b_rows = jnp.repeat(jnp.swapaxes(a_b_s, 1, 2), GROUP_DIM, axis=2)
    norm_mix_r, norm_ffn_r, f_conv_b_r = row(norm_mix), row(norm_ffn), row(f_conv_b)
    a_v_gain_r, b_scale_r = row(a_v_gain), row(b_scale)
    c_conv_b_r, c_ln_g_r, c_ln_b_r = row(c_conv_b), row(c_ln_g), row(c_ln_b)
    c_conv_w_t = jnp.broadcast_to(c_conv_w[:, :, None, :], c_conv_w.shape[:2] + (F32_ROWS, c_conv_w.shape[2]))
    final_gain = norm_final.reshape(1, d)

    def trunk(x):
        bsz, seq, _ = x.shape
        h = x.reshape(bsz * seq, d)
        tables = _dft_tables(seq) if depth > 1 else None
        for l in range(depth):
            i = l // 2
            if l % 2 == 0:
                h = _even_layer(h, norm_mix_r, l, ab_w_in_b, a_v_gain_r, a_w_s_b, b_rows, b_w_g_b, b_scale_r,
                                ab_w_out_b, i, seq, tm=MIXER_TM)
            else:
                yc, xd = _odd_front(h, norm_mix_r, l, cd_w_in_b, c_conv_w_t, c_conv_b_r, c_ln_g_r, c_ln_b_r,
                                    i, seq, tm=MIXER_TM)
                yd = _mixer_d(xd, bsz, seq, tables)
                h = _proj_out(h, yc, yd, cd_w_out_b, i, tm=MIXER_TM)
            h = _ffn(h, norm_ffn_r, f_w_in_b, f_conv_w, f_conv_b_r, f_w_out_b, final_gain, l, seq,
                     tm=FFN_TM, tf=FFN_TF, final=(l == depth - 1))
        return h.reshape(bsz, seq, d)

    return trunk(x_prompt), trunk(x_sample)
```

```python
import functools

import numpy as np
import jax
import jax.numpy as jnp
from jax import lax
from jax.experimental import pallas as pl
from jax.experimental.pallas import tpu as pltpu

F32 = jnp.float32
BF16 = jnp.bfloat16
EPS = 1e-6

CHUNK = 128
A_HEADS = 4
POOL_WINDOWS = (2, 4, 8, 16)
GROUP_DIM = 256
DFT_N1 = 128
CONV_ROW_BLOCKS = 4
COL_BLOCK = 256

VMEM_LIMIT = 56 * 1024 * 1024
BF16_ROWS = 16
F32_ROWS = 8

MIXER_TM = 512
FFN_TM = 1024
FFN_TF = 512
FFN_ROW_PARTS = 2


def _rms(x, g):
    return x * lax.rsqrt(jnp.mean(x * x, axis=-1, keepdims=True) + EPS) * g


def _params(sem):
    return pltpu.CompilerParams(dimension_semantics=sem, vmem_limit_bytes=VMEM_LIMIT)


def _halo_specs(tm, m, d, halo):
    hb = tm // halo
    last = m // halo - 1
    return (pl.BlockSpec((halo, d), lambda i: (jnp.maximum(i * hb - 1, 0), 0)),
            pl.BlockSpec((halo, d), lambda i: (jnp.minimum((i + 1) * hb, last), 0)))


def _resident(shape, index_map):
    return pl.BlockSpec(shape, index_map, pipeline_mode=pl.Buffered(1))


def _norm_with_halo(hbuf, xp_ref, x_ref, xn_ref, g, it, tps):
    halo = xp_ref.shape[0]
    tm = x_ref.shape[0]
    hbuf[halo:halo + tm, :] = _rms(x_ref[...], g).astype(BF16)
    hbuf[0:halo, :] = jnp.where(it == 0, 0.0, _rms(xp_ref[...], g)).astype(BF16)
    hbuf[halo + tm:2 * halo + tm, :] = jnp.where(it == tps - 1, 0.0, _rms(xn_ref[...], g)).astype(BF16)


def _even_layer_kernel(xp_ref, x_ref, xn_ref, g_ref, w_ref, vg_ref, ws_ref, bm_ref, wg_ref, sc_ref, wo_ref,
                       o_ref, hbuf, ubuf, vbuf, xbuf, ycat, *, tm, tps, seq):
    it = pl.program_id(0) % tps
    halo = xp_ref.shape[0]
    aw = ubuf.shape[1]
    _norm_with_halo(hbuf, xp_ref, x_ref, xn_ref, g_ref[...], it, tps)
    main = slice(halo, halo + tm)
    vbuf[...] = jnp.dot(hbuf[main, :], w_ref[:, aw:2 * aw], preferred_element_type=F32)
    xbuf[...] = jnp.dot(hbuf[...], w_ref[:, 2 * aw:3 * aw], preferred_element_type=F32)
    ubuf[...] = jnp.dot(hbuf[main, :], w_ref[:, 0:aw], preferred_element_type=F32)

    for c in range(tm // CHUNK):
        rows = slice(c * CHUNK, (c + 1) * CHUNK)
        vn = _rms(jax.nn.gelu(vbuf[rows, :]), vg_ref[...]).astype(BF16)
        for h in range(A_HEADS):
            cols = slice(h * GROUP_DIM, (h + 1) * GROUP_DIM)
            mixed = jnp.dot(ws_ref[h], vn[:, cols], preferred_element_type=F32)
            ycat[rows, cols] = (jax.nn.gelu(ubuf[rows, cols]) * (mixed + bm_ref[:, cols])).astype(BF16)

    pos = it * tm + lax.broadcasted_iota(jnp.int32, (tm, GROUP_DIM), 0)
    for g, w in enumerate(POOL_WINDOWS):
        half = w // 2
        cols = slice(g * GROUP_DIM, (g + 1) * GROUP_DIM)
        s = xbuf[pl.ds(halo - half, tm), cols]
        for dlt in range(-half + 1, half):
            s = s + xbuf[pl.ds(halo + dlt, tm), cols]
        cnt = (jnp.minimum(pos + half, seq) - jnp.maximum(pos - half, 0)).astype(F32)
        pooled = (s / cnt - xbuf[main, cols]).astype(BF16)
        mixed = jnp.dot(pooled, wg_ref[g], preferred_element_type=F32)
        ycat[:, aw + g * GROUP_DIM:aw + (g + 1) * GROUP_DIM] = (mixed * sc_ref[:, cols]).astype(BF16)

    o_ref[...] = x_ref[...] + jnp.dot(ycat[...], wo_ref[...], preferred_element_type=F32)


def _even_layer(x, gain, norm_idx, w_in, v_gain, w_s, b_rows, w_g, scale, w_out, layer, seq, *, tm):
    m, d = x.shape
    aw = w_in.shape[2] // 3
    halo = BF16_ROWS
    assert seq % tm == 0 and m % seq == 0 and tm % CHUNK == 0 and max(POOL_WINDOWS) // 2 <= halo
    assert aw == A_HEADS * GROUP_DIM == len(POOL_WINDOWS) * GROUP_DIM
    prev, nxt = _halo_specs(tm, m, d, halo)
    kern = functools.partial(_even_layer_kernel, tm=tm, tps=seq // tm, seq=seq)
    return pl.pallas_call(
        kern,
        grid=(m // tm,),
        in_specs=[
            prev,
            pl.BlockSpec((tm, d), lambda i: (i, 0)),
            nxt,
            _resident((None, 1, d), lambda i: (norm_idx, 0, 0)),
            _resident((None, d, 3 * aw), lambda i: (layer, 0, 0)),
            _resident((None, 1, aw), lambda i: (layer, 0, 0)),
            _resident((None, A_HEADS, CHUNK, CHUNK), lambda i: (layer, 0, 0, 0)),
            _resident((None, CHUNK, aw), lambda i: (layer, 0, 0)),
            _resident((None, len(POOL_WINDOWS), GROUP_DIM, GROUP_DIM), lambda i: (layer, 0, 0, 0)),
            _resident((None, 1, aw), lambda i: (layer, 0, 0)),
            _resident((None, 2 * aw, d), lambda i: (layer, 0, 0)),
        ],
        out_specs=pl.BlockSpec((tm, d), lambda i: (i, 0)),
        out_shape=jax.ShapeDtypeStruct((m, d), F32),
        scratch_shapes=[
            pltpu.VMEM((tm + 2 * halo, d), BF16),
            pltpu.VMEM((tm, aw), F32),
            pltpu.VMEM((tm, aw), F32),
            pltpu.VMEM((tm + 2 * halo, aw), F32),
            pltpu.VMEM((tm, 2 * aw), BF16),
        ],
        compiler_params=_params(("parallel",)),
        name="even_layer",
    )(x, x, x, gain, w_in, v_gain, w_s, b_rows, w_g, scale, w_out)


def _odd_front_kernel(xp_ref, x_ref, xn_ref, g_ref, w_ref, cw_ref, cb_ref, lg_ref, lb_ref,
                      yc_ref, xd_ref, hbuf, ysh, cbuf, *, tm, tps, cblk):
    it = pl.program_id(0) % tps
    halo = xp_ref.shape[0]
    wdt = yc_ref.shape[1]
    taps = cw_ref.shape[0]
    rows = F32_ROWS
    _norm_with_halo(hbuf, xp_ref, x_ref, xn_ref, g_ref[...], it, tps)
    xd_ref[...] = jnp.dot(hbuf[halo:halo + tm, :], w_ref[:, 2 * wdt:3 * wdt], preferred_element_type=F32)

    span = tm + 2 * halo - rows
    first = halo - taps // 2
    for cb in range(wdt // cblk):
        cols = slice(cb * cblk, (cb + 1) * cblk)
        gcols = slice(wdt + cb * cblk, wdt + (cb + 1) * cblk)
        a = jnp.dot(hbuf[...], w_ref[:, cols], preferred_element_type=F32)
        gate = jnp.dot(hbuf[...], w_ref[:, gcols], preferred_element_type=F32)
        buf = ysh.at[cb % ysh.shape[0]]
        buf[0, :, :] = a * jax.nn.sigmoid(gate)
        for r in range(1, rows):
            buf[r, 0:span, :] = buf[0, pl.ds(r, span), :]
        for blk0 in range(0, tm // rows, CONV_ROW_BLOCKS):
            accs = [None] * CONV_ROW_BLOCKS
            for k in range(taps):
                q, r = divmod(first + k, rows)
                wk = cw_ref[k, :, cols]
                for j in range(CONV_ROW_BLOCKS):
                    term = wk * buf[r, pl.ds((blk0 + j + q) * rows, rows), :]
                    accs[j] = term if accs[j] is None else accs[j] + term
            for j in range(CONV_ROW_BLOCKS):
                cbuf[(blk0 + j) * rows:(blk0 + j + 1) * rows, cols] = accs[j]

    grp = BF16_ROWS
    for blk in range(tm // grp):
        rsl = slice(blk * grp, (blk + 1) * grp)
        y = cbuf[rsl, :] + cb_ref[...]
        yc = y - jnp.mean(y, axis=-1, keepdims=True)
        ln = yc * lax.rsqrt(jnp.mean(yc * yc, axis=-1, keepdims=True) + EPS) * lg_ref[...] + lb_ref[...]
        yc_ref[rsl, :] = (ln * jax.nn.sigmoid(ln)).astype(BF16)


def _odd_front(x, gain, norm_idx, w_in, conv_w, conv_b, ln_g, ln_b, layer, seq, *, tm, cblk=COL_BLOCK):
    m, d = x.shape
    wdt = w_in.shape[2] // 3
    taps = conv_w.shape[1]
    halo = BF16_ROWS
    assert taps // 2 <= halo and seq % tm == 0 and m % seq == 0 and wdt % cblk == 0
    assert (tm // F32_ROWS) % CONV_ROW_BLOCKS == 0
    prev, nxt = _halo_specs(tm, m, d, halo)
    kern = functools.partial(_odd_front_kernel, tm=tm, tps=seq // tm, cblk=cblk)
    vec = _resident((None, 1, wdt), lambda i: (layer, 0, 0))
    return pl.pallas_call(
        kern,
        grid=(m // tm,),
        in_specs=[
            prev,
            pl.BlockSpec((tm, d), lambda i: (i, 0)),
            nxt,
            _resident((None, 1, d), lambda i: (norm_idx, 0, 0)),
            _resident((None, d, 3 * wdt), lambda i: (layer, 0, 0)),
            _resident((None, taps, F32_ROWS, wdt), lambda i: (layer, 0, 0, 0)),
            vec, vec, vec,
        ],
        out_specs=[pl.BlockSpec((tm, wdt), lambda i: (i, 0)), pl.BlockSpec((tm, wdt), lambda i: (i, 0))],
        out_shape=[jax.ShapeDtypeStruct((m, wdt), BF16), jax.ShapeDtypeStruct((m, wdt), F32)],
        scratch_shapes=[
            pltpu.VMEM((tm + 2 * halo, d), BF16),
            pltpu.VMEM((2, F32_ROWS, tm + 2 * halo, cblk), F32),
            pltpu.VMEM((tm, wdt), F32),
        ],
        compiler_params=_params(("parallel",)),
        name="odd_front",
    )(x, x, x, gain, w_in, conv_w, conv_b, ln_g, ln_b)


def _proj_out_kernel(x_ref, y1_ref, y2_ref, w1_ref, w2_ref, o_ref):
    acc = jnp.dot(y1_ref[...].astype(BF16), w1_ref[...], preferred_element_type=F32)
    acc += jnp.dot(y2_ref[...].astype(BF16), w2_ref[...], preferred_element_type=F32)
    o_ref[...] = x_ref[...] + acc


def _proj_out(x, y1, y2, w, layer, *, tm):
    m, d = x.shape
    kh = w.shape[1] // 2
    return pl.pallas_call(
        _proj_out_kernel,
        grid=(m // tm,),
        in_specs=[
            pl.BlockSpec((tm, d), lambda i: (i, 0)),
            pl.BlockSpec((tm, kh), lambda i: (i, 0)),
            pl.BlockSpec((tm, kh), lambda i: (i, 0)),
            _resident((None, kh, d), lambda i: (layer, 0, 0)),
            _resident((None, kh, d), lambda i: (layer, 1, 0)),
        ],
        out_specs=pl.BlockSpec((tm, d), lambda i: (i, 0)),
        out_shape=jax.ShapeDtypeStruct((m, d), F32),
        compiler_params=_params(("parallel",)),
        name="proj_out",
    )(x, y1, y2, w, w)


def _dft_tables(seq):
    n1 = DFT_N1
    n2 = seq // n1
    c = np.arange(GROUP_DIM)
    ang = 2.0 * np.pi * ((c[:, None] * c[None, :]) % GROUP_DIM) / GROUP_DIM
    fc = jnp.asarray(np.concatenate([np.cos(ang), -np.sin(ang)], axis=1), BF16)

    shape = (n2, n1, n1)
    r = lax.broadcasted_iota(jnp.int32, shape, 0)
    k1 = lax.broadcasted_iota(jnp.int32, shape, 1)
    j = lax.broadcasted_iota(jnp.int32, shape, 2)
    idx = (k1 * (n2 * j + r)) % seq
    a1 = idx.astype(F32) * (2.0 * np.pi / seq)
    gr, gi = jnp.cos(a1), -jnp.sin(a1)
    m1 = jnp.concatenate([jnp.concatenate([gr, -gi], axis=2),
                          jnp.concatenate([gi, gr], axis=2)], axis=1).astype(BF16)

    k2 = np.arange(n2)
    a2 = 2.0 * np.pi * ((k2[:, None] * k2[None, :]) % n2) / n2
    norm = 1.0 / np.sqrt(float(seq) * GROUP_DIM)
    wc = jnp.asarray(np.cos(a2) * norm, BF16)
    ws = jnp.asarray(np.sin(a2) * norm, BF16)
    return fc, m1, wc, ws


def _dft1_kernel(x_ref, fc_ref, m1_ref, o_ref, zbuf):
    n1, rt, wdt = x_ref.shape
    xt = pltpu.einshape("jrc->rjc", x_ref[...])
    for rr in range(rt):
        x = xt[rr].astype(BF16)
        for g in range(x_ref.shape[2] // GROUP_DIM):
            cols = slice(g * GROUP_DIM, (g + 1) * GROUP_DIM)
            zz = jnp.dot(x[:, cols], fc_ref[...], preferred_element_type=F32)
            zbuf[0:n1, cols] = zz[:, :GROUP_DIM].astype(BF16)
            zbuf[n1:2 * n1, cols] = zz[:, GROUP_DIM:].astype(BF16)
        y = jnp.dot(m1_ref[rr], zbuf[...], preferred_element_type=F32)
        o_ref[rr] = _pack_bf16_pair(y[:n1], y[n1:])


def _pack_bf16_pair(hi, lo):
    hi_bits = lax.bitcast_convert_type(hi.astype(BF16).astype(F32), jnp.uint32)
    lo_bits = lax.bitcast_convert_type(lo.astype(BF16).astype(F32), jnp.uint32)
    return hi_bits | (lo_bits >> 16)


def _unpack_bf16_pair(word):
    hi = lax.bitcast_convert_type(word & jnp.uint32(0xFFFF0000), F32)
    lo = lax.bitcast_convert_type(word << 16, F32)
    return hi.astype(BF16), lo.astype(BF16)


def _dft2_kernel(wc_ref, ws_ref, y_ref, o_ref):
    rt = y_ref.shape[1]
    y_t = pltpu.einshape("rkc->krc", y_ref[...])
    res = []
    for kk in range(rt):
        yr, yi = _unpack_bf16_pair(y_t[kk])
        res.append(jnp.dot(wc_ref[...], yr, preferred_element_type=F32)
                   + jnp.dot(ws_ref[...], yi, preferred_element_type=F32))
    o_ref[...] = pltpu.einshape("krc->rkc", jnp.stack(res, axis=0))


def _mixer_d(xd, bsz, seq, tables):
    fc, m1, wc, ws = tables
    m, wdt = xd.shape
    n1 = DFT_N1
    n2 = seq // n1
    rt = F32_ROWS
    assert seq == n1 * n2 and n2 % rt == 0 and wdt % GROUP_DIM == 0
    pv = xd.reshape(bsz, n1, n2 // rt, rt, wdt)
    yp = pl.pallas_call(
        _dft1_kernel,
        grid=(bsz, n2 // rt),
        in_specs=[
            pl.BlockSpec((None, n1, None, rt, wdt), lambda b, r: (b, 0, r, 0, 0)),
            pl.BlockSpec((GROUP_DIM, 2 * GROUP_DIM), lambda b, r: (0, 0)),
            pl.BlockSpec((rt, 2 * n1, 2 * n1), lambda b, r: (r, 0, 0)),
        ],
        out_specs=pl.BlockSpec((None, rt, n1, wdt), lambda b, r: (b, r, 0, 0)),
        out_shape=jax.ShapeDtypeStruct((bsz, n2, n1, wdt), jnp.uint32),
        scratch_shapes=[pltpu.VMEM((2 * n1, wdt), BF16)],
        compiler_params=_params(("parallel", "parallel")),
        name="dft_stage1",
    )(pv, fc, m1)
    out = pl.pallas_call(
        _dft2_kernel,
        grid=(bsz, n1 // rt),
        in_specs=[
            pl.BlockSpec((n2, n2), lambda b, c: (0, 0)),
            pl.BlockSpec((n2, n2), lambda b, c: (0, 0)),
            pl.BlockSpec((None, n2, rt, wdt), lambda b, c: (b, 0, c, 0)),
        ],
        out_specs=pl.BlockSpec((None, n2, rt, wdt), lambda b, c: (b, 0, c, 0)),
        out_shape=jax.ShapeDtypeStruct((bsz, n2, n1, wdt), F32),
        compiler_params=_params(("parallel", "parallel")),
        name="dft_stage2",
    )(wc, ws, yp)
    return out.reshape(m, wdt)


def _ffn_kernel(xp_ref, x_ref, xn_ref, g_ref, wa_ref, wg_ref, cw_ref, cb_ref, wo_ref, gf_ref,
                o_ref, hbuf, abuf, *, tm, tps, final):
    it = pl.program_id(0) % tps
    f = pl.program_id(1)
    halo = xp_ref.shape[0]

    @pl.when(f == 0)
    def _():
        g = g_ref[...]
        x = x_ref[...]
        hbuf[halo:halo + tm, :] = _rms(x, g).astype(BF16)
        hbuf[0:halo, :] = jnp.where(it == 0, 0.0, _rms(xp_ref[...], g)).astype(BF16)
        hbuf[halo + tm:2 * halo + tm, :] = jnp.where(it == tps - 1, 0.0, _rms(xn_ref[...], g)).astype(BF16)
        o_ref[...] = x

    part = tm // FFN_ROW_PARTS
    gates = []
    for p in range(FFN_ROW_PARTS):
        lo = 0 if p == 0 else halo + p * part
        hi = tm + 2 * halo if p == FFN_ROW_PARTS - 1 else halo + (p + 1) * part
        abuf[lo:hi, :] = jnp.dot(hbuf[lo:hi, :], wa_ref[...], preferred_element_type=F32)
        rows = slice(halo + p * part, halo + (p + 1) * part)
        gates.append(jnp.dot(hbuf[rows, :], wg_ref[...], preferred_element_type=F32))
    for p, gate in enumerate(gates):
        r0 = p * part
        a = (cw_ref[0:1, :] * abuf[pl.ds(halo + r0 - 1, part), :]
             + cw_ref[1:2, :] * abuf[pl.ds(halo + r0, part), :]
             + cw_ref[2:3, :] * abuf[pl.ds(halo + r0 + 1, part), :]) + cb_ref[...]
        act = (jax.nn.gelu(a) * gate).astype(BF16)
        o_ref[r0:r0 + part, :] += jnp.dot(act, wo_ref[...], preferred_element_type=F32)

    if final:
        @pl.when(f == pl.num_programs(1) - 1)
        def _():
            o_ref[...] = _rms(o_ref[...], gf_ref[...])


def _ffn(x, gain, w_in, conv_w, conv_b, w_out, final_gain, layer, seq, *, tm, tf, final):
    m, d = x.shape
    dff = w_out.shape[1]
    nf = dff // tf
    tps = seq // tm
    halo = BF16_ROWS
    assert seq % tm == 0 and m % seq == 0 and dff % tf == 0 and w_in.shape[2] == 2 * dff
    hb = tm // halo
    last = m // halo - 1
    kern = functools.partial(_ffn_kernel, tm=tm, tps=tps, final=final)
    return pl.pallas_call(
        kern,
        grid=(m // tm, nf),
        in_specs=[
            pl.BlockSpec((halo, d), lambda i, f: (jnp.maximum(i * hb - 1, 0), 0)),
            pl.BlockSpec((tm, d), lambda i, f: (i, 0)),
            pl.BlockSpec((halo, d), lambda i, f: (jnp.minimum((i + 1) * hb, last), 0)),
            pl.BlockSpec((None, 1, d), lambda i, f: (layer, 0, 0)),
            pl.BlockSpec((None, d, tf), lambda i, f: (layer, 0, f)),
            pl.BlockSpec((None, d, tf), lambda i, f: (layer, 0, f + nf)),
            pl.BlockSpec((None, conv_w.shape[1], tf), lambda i, f: (layer, 0, f)),
            pl.BlockSpec((None, 1, tf), lambda i, f: (layer, 0, f)),
            pl.BlockSpec((None, tf, d), lambda i, f: (layer, f, 0)),
            pl.BlockSpec((1, d), lambda i, f: (0, 0)),
        ],
        out_specs=pl.BlockSpec((tm, d), lambda i, f: (i, 0)),
        out_shape=jax.ShapeDtypeStruct((m, d), F32),
        scratch_shapes=[
            pltpu.VMEM((tm + 2 * halo, d), BF16),
            pltpu.VMEM((tm + 2 * halo, tf), F32),
        ],
        compiler_params=_params(("parallel", "arbitrary")),
        name="ffn",
    )(x, x, x, gain, w_in, w_in, conv_w, conv_b, w_out, final_gain)


def kernel(x_prompt, x_sample, norm_mix, norm_ffn, norm_final, ab_w_in, ab_w_out, a_v_gain, a_w_s, a_b_s,
           b_w_g, b_scale, cd_w_in, cd_w_out, c_conv_w, c_conv_b, c_ln_g, c_ln_b, f_w_in, f_conv_w,
           f_conv_b, f_w_out):
    depth, d = norm_mix.shape
    row = lambda v: v.reshape(v.shape[0], 1, v.shape[1])
    ab_w_in_b, ab_w_out_b = ab_w_in.astype(BF16), ab_w_out.astype(BF16)
    cd_w_in_b, cd_w_out_b = cd_w_in.astype(BF16), cd_w_out.astype(BF16)
    f_w_in_b, f_w_out_b = f_w_in.astype(BF16), f_w_out.astype(BF16)
    a_w_s_b, b_w_g_b = a_w_s.astype(BF16), b_w_g.astype(BF16)
    b_rows = jnp.repeat(jnp.swapaxes(a_b_s, 1, 2), GROUP_DIM, axis=2)
    norm_mix_r, norm_ffn_r, f_conv_b_r = row(norm_mix), row(norm_ffn), row(f_conv_b)
    a_v_gain_r, b_scale_r = row(a_v_gain), row(b_scale)
    c_conv_b_r, c_ln_g_r, c_ln_b_r = row(c_conv_b), row(c_ln_g), row(c_ln_b)
    c_conv_w_t = jnp.broadcast_to(c_conv_w[:, :, None, :], c_conv_w.shape[:2] + (F32_ROWS, c_conv_w.shape[2]))
    final_gain = norm_final.reshape(1, d)

    def trunk(x):
        bsz, seq, _ = x.shape
        h = x.reshape(bsz * seq, d)
        tables = _dft_tables(seq) if depth > 1 else None
        for l in range(depth):
            i = l // 2
            if l % 2 == 0:
                h = _even_layer(h, norm_mix_r, l, ab_w_in_b, a_v_gain_r, a_w_s_b, b_rows, b_w_g_b, b_scale_r,
                                ab_w_out_b, i, seq, tm=MIXER_TM)
            else:
                yc, xd = _odd_front(h, norm_mix_r, l, cd_w_in_b, c_conv_w_t, c_conv_b_r, c_ln_g_r, c_ln_b_r,
                                    i, seq, tm=MIXER_TM)
                yd = _mixer_d(xd, bsz, seq, tables)
                h = _proj_out(h, yc, yd, cd_w_out_b, i, tm=MIXER_TM)
            h = _ffn(h, norm_ffn_r, f_w_in_b, f_conv_w, f_conv_b_r, f_w_out_b, final_gain, l, seq,
                     tm=FFN_TM, tf=FFN_TF, final=(l == depth - 1))
        return h.reshape(bsz, seq, d)

    return trunk(x_prompt), trunk(x_sample)
```
